```python
import jax, jax.numpy as jnp
from jax import lax
import numpy as np

D_MODEL = 1024
BATCH = 32
SEQ = 2048
DEPTH = 2
DEC_BATCH = 16
DEC_SEQ = 16
PAST_LEN = 4096

CHUNK = 64
N_A_LAYERS = DEPTH // 2
N_B_LAYERS = DEPTH - N_A_LAYERS
LRU_WIDTH = D_MODEL
N_LRU_BLOCKS = 16
LRU_BLOCK = LRU_WIDTH // N_LRU_BLOCKS
LRU_CONV = 4
LRU_C = 8.0
N_HEADS = 8
HEAD_DIM = D_MODEL // N_HEADS
D_FF = 3 * D_MODEL
FFN_CONV = 3
Q_BLOCK = 128
EPS = 1e-6

kernel_name = 'hawk_stickbreak_yoco_stream_step'


def rmsnorm(x, g):
    xf = x.astype(jnp.float32)
    y = xf * lax.rsqrt(jnp.mean(xf * xf, axis=-1, keepdims=True) + EPS) * g.astype(jnp.float32)
    return y.astype(x.dtype)


def causal_dwconv(x, prev, w, b):
    width = w.shape[0]
    t_len = x.shape[1]
    xp = jnp.concatenate([prev.astype(x.dtype), x], axis=1)
    y = b
    for k in range(width):
        y = y + w[k] * xp[:, k:k + t_len]
    return y.astype(x.dtype), xp[:, xp.shape[1] - (width - 1):]


def linear_scan(a, b, h0):
    b = b.at[:, 0].add(a[:, 0] * h0)
    def comb(l, r):
        return (l[0] * r[0], r[0] * l[1] + r[1])
    _, h = lax.associative_scan(comb, (a, b), axis=1)
    return h


def rglru_mixer(x, h0, conv_prev, norm, w_in, conv_w, conv_b, wr, br, wi, bi, lam, w_out):
    bsz, t_len, _ = x.shape
    xn = rmsnorm(x, norm)
    gate_in, rec_in = jnp.split(xn @ w_in, 2, axis=-1)
    c, conv_new = causal_dwconv(rec_in, conv_prev, conv_w, conv_b)
    cb = c.reshape(bsz, t_len, N_LRU_BLOCKS, LRU_BLOCK)
    r = jax.nn.sigmoid(jnp.einsum('btnd,nde->btne', cb, wr).reshape(bsz, t_len, LRU_WIDTH) + br).astype(jnp.float32)
    i = jax.nn.sigmoid(jnp.einsum('btnd,nde->btne', cb, wi).reshape(bsz, t_len, LRU_WIDTH) + bi).astype(jnp.float32)
    log_a = LRU_C * r * jax.nn.log_sigmoid(lam.astype(jnp.float32))
    a = jnp.exp(log_a)
    bterm = jnp.sqrt(-jnp.expm1(2.0 * log_a)) * i * c.astype(jnp.float32)
    h = linear_scan(a, bterm, h0.astype(jnp.float32))
    y = (h.astype(x.dtype) * jax.nn.gelu(gate_in)) @ w_out
    return y, h[:, -1].astype(x.dtype), conv_new


def shared_kv(x, kv_norm, w_kv, k_norm):
    bsz, t_len, _ = x.shape
    k, v = jnp.split(rmsnorm(x, kv_norm) @ w_kv, 2, axis=-1)
    k = rmsnorm(k.reshape(bsz, t_len, N_HEADS, HEAD_DIM), k_norm)
    return k, v.reshape(bsz, t_len, N_HEADS, HEAD_DIM)


def sb_attention(q, k, v):
    t_q = q.shape[1]
    off = k.shape[1] - t_q
    scale = HEAD_DIM ** -0.5
    outs = []
    for qs in range(0, t_q, Q_BLOCK):
        qe = min(qs + Q_BLOCK, t_q)
        kl = off + qe
        z = jnp.einsum('bqhd,bshd->bhqs', q[:, qs:qe], k[:, :kl], preferred_element_type=jnp.float32) * scale
        qpos = off + qs + jnp.arange(qe - qs)
        mask = jnp.arange(kl)[None, :] < qpos[:, None]
        log_beta = jax.nn.log_sigmoid(z)
        log_stay = jnp.where(mask, jax.nn.log_sigmoid(-z), 0.0)
        log_after = lax.cumsum(log_stay, axis=3, reverse=True) - log_stay
        w = jnp.where(mask, jnp.exp(log_beta + log_after), 0.0)
        outs.append(jnp.einsum('bhqs,bshd->bqhd', w.astype(v.dtype), v[:, :kl]))
    return jnp.concatenate(outs, axis=1)


def sb_mixer(x, k_all, v_all, norm, wq, q_norm, wo):
    bsz, t_len, _ = x.shape
    q = rmsnorm((rmsnorm(x, norm) @ wq).reshape(bsz, t_len, N_HEADS, HEAD_DIM), q_norm)
    o = sb_attention(q, k_all, v_all)
    return o.reshape(bsz, t_len, N_HEADS * HEAD_DIM) @ wo


def conv_ffn(x, prev, norm, w_up, conv_w, conv_b, w_down):
    g, u = jnp.split(rmsnorm(x, norm) @ w_up, 2, axis=-1)
    gc, new = causal_dwconv(g, prev, conv_w, conv_b)
    return (jax.nn.gelu(gc) * u) @ w_down, new


def trunk(x, lru_h, lru_conv, ffn_conv, cache_k, cache_v,
          a_norm, a_w_in, a_conv_w, a_conv_b, a_wr, a_br, a_wi, a_bi, a_lambda, a_w_out,
          kv_norm, w_kv, k_norm, b_norm, b_wq, q_norm, b_wo,
          f_norm, f_w_up, f_conv_w, f_conv_b, f_w_down, out_norm):
    h_out, c_out, f_out = [], [], []
    for l in range(DEPTH):
        if l < N_A_LAYERS:
            y, h_l, c_l = rglru_mixer(x, lru_h[l], lru_conv[l], a_norm[l], a_w_in[l], a_conv_w[l], a_conv_b[l],
                                      a_wr[l], a_br[l], a_wi[l], a_bi[l], a_lambda[l], a_w_out[l])
            h_out.append(h_l)
            c_out.append(c_l)
        else:
            if l == N_A_LAYERS:
                k_new, v_new = shared_kv(x, kv_norm, w_kv, k_norm)
                k_all = jnp.concatenate([cache_k.astype(x.dtype), k_new], axis=1)
                v_all = jnp.concatenate([cache_v.astype(x.dtype), v_new], axis=1)
            j = l - N_A_LAYERS
            y = sb_mixer(x, k_all, v_all, b_norm[j], b_wq[j], q_norm[j], b_wo[j])
        x = x + y
        y, f_l = conv_ffn(x, ffn_conv[l], f_norm[l], f_w_up[l], f_conv_w[l], f_conv_b[l], f_w_down[l])
        f_out.append(f_l)
        x = x + y
    return rmsnorm(x, out_norm), jnp.stack(h_out), jnp.stack(c_out), jnp.stack(f_out), k_new, v_new


def setup_inputs(seed: int = 0) -> dict:
    key = jax.random.key(seed)
    ks = jax.random.split(key, 40)
    f32 = jnp.float32
    def nrm(k, shape, s):
        return jax.random.normal(k, shape, f32) * s
    na, nb = N_A_LAYERS, N_B_LAYERS
    a0 = jax.random.uniform(ks[14], (na, LRU_WIDTH), f32, 0.9, 0.999)
    root = a0 ** (1.0 / LRU_C)
    a_lambda = jnp.log(root) - jnp.log1p(-root)
    return {
        'x_prompt': nrm(ks[0], (BATCH, SEQ, D_MODEL), 1.0),
        'x_sample': nrm(ks[1], (DEC_BATCH, DEC_SEQ, D_MODEL), 1.0),
        'state_lru_h': nrm(ks[2], (na, DEC_BATCH, LRU_WIDTH), 0.5),
        'state_lru_conv': nrm(ks[3], (na, DEC_BATCH, LRU_CONV - 1, LRU_WIDTH), 1.0),
        'state_ffn_conv': nrm(ks[4], (DEPTH, DEC_BATCH, FFN_CONV - 1, D_FF), 1.0),
        'cache_k': nrm(ks[5], (DEC_BATCH, PAST_LEN, N_HEADS, HEAD_DIM), 1.0),
        'cache_v': nrm(ks[6], (DEC_BATCH, PAST_LEN, N_HEADS, HEAD_DIM), 1.0),
        'a_norm': 1.0 + nrm(ks[7], (na, D_MODEL), 0.01),
        'a_w_in': nrm(ks[8], (na, D_MODEL, 2 * LRU_WIDTH), D_MODEL ** -0.5),
        'a_conv_w': nrm(ks[9], (na, LRU_CONV, LRU_WIDTH), LRU_CONV ** -0.5),
        'a_conv_b': nrm(ks[10], (na, LRU_WIDTH), 0.01),
        'a_wr': nrm(ks[11], (na, N_LRU_BLOCKS, LRU_BLOCK, LRU_BLOCK), LRU_BLOCK ** -0.5),
        'a_br': nrm(ks[12], (na, LRU_WIDTH), 0.01),
        'a_wi': nrm(ks[13], (na, N_LRU_BLOCKS, LRU_BLOCK, LRU_BLOCK), LRU_BLOCK ** -0.5),
        'a_bi': nrm(ks[15], (na, LRU_WIDTH), 0.01),
        'a_lambda': a_lambda,
        'a_w_out': nrm(ks[16], (na, LRU_WIDTH, D_MODEL), LRU_WIDTH ** -0.5),
        'kv_norm': 1.0 + nrm(ks[17], (D_MODEL,), 0.01),
        'w_kv': nrm(ks[18], (D_MODEL, 2 * N_HEADS * HEAD_DIM), D_MODEL ** -0.5),
        'k_norm': 1.0 + nrm(ks[19], (HEAD_DIM,), 0.01),
        'b_norm': 1.0 + nrm(ks[20], (nb, D_MODEL), 0.01),
        'b_wq': nrm(ks[21], (nb, D_MODEL, N_HEADS * HEAD_DIM), D_MODEL ** -0.5),
        'q_norm': 1.0 + nrm(ks[22], (nb, HEAD_DIM), 0.01),
        'b_wo': nrm(ks[23], (nb, N_HEADS * HEAD_DIM, D_MODEL), (N_HEADS * HEAD_DIM) ** -0.5),
        'f_norm': 1.0 + nrm(ks[24], (DEPTH, D_MODEL), 0.01),
        'f_w_up': nrm(ks[25], (DEPTH, D_MODEL, 2 * D_FF), D_MODEL ** -0.5),
        'f_conv_w': nrm(ks[26], (DEPTH, FFN_CONV, D_FF), FFN_CONV ** -0.5),
        'f_conv_b': nrm(ks[27], (DEPTH, D_FF), 0.01),
        'f_w_down': nrm(ks[28], (DEPTH, D_FF, D_MODEL), D_FF ** -0.5),
        'out_norm': 1.0 + nrm(ks[29], (D_MODEL,), 0.01),
    }


def reference(x_prompt, x_sample, state_lru_h, state_lru_conv, state_ffn_conv, cache_k, cache_v,
              a_norm, a_w_in, a_conv_w, a_conv_b, a_wr, a_br, a_wi, a_bi, a_lambda, a_w_out,
              kv_norm, w_kv, k_norm, b_norm, b_wq, q_norm, b_wo,
              f_norm, f_w_up, f_conv_w, f_conv_b, f_w_down, out_norm):
    assert x_sample.shape[1] <= CHUNK
    weights = (a_norm, a_w_in, a_conv_w, a_conv_b, a_wr, a_br, a_wi, a_bi, a_lambda, a_w_out,
               kv_norm, w_kv, k_norm, b_norm, b_wq, q_norm, b_wo,
               f_norm, f_w_up, f_conv_w, f_conv_b, f_w_down, out_norm)
    bp = x_prompt.shape[0]
    dt = x_prompt.dtype
    y_prompt, p_lru_h, p_lru_conv, p_ffn_conv, p_k, p_v = trunk(
        x_prompt,
        jnp.zeros((N_A_LAYERS, bp, LRU_WIDTH), dt),
        jnp.zeros((N_A_LAYERS, bp, LRU_CONV - 1, LRU_WIDTH), dt),
        jnp.zeros((DEPTH, bp, FFN_CONV - 1, D_FF), dt),
        jnp.zeros((bp, 0, N_HEADS, HEAD_DIM), dt),
        jnp.zeros((bp, 0, N_HEADS, HEAD_DIM), dt),
        *weights)
    y_sample, s_lru_h, s_lru_conv, s_ffn_conv, s_k, s_v = trunk(
        x_sample, state_lru_h, state_lru_conv, state_ffn_conv, cache_k, cache_v, *weights)
    return (y_prompt, y_sample, p_lru_h, p_lru_conv, p_ffn_conv, p_k, p_v,
            s_lru_h, s_lru_conv, s_ffn_conv, s_k, s_v)
```

```python
import functools
import math

import jax
import jax.numpy as jnp
from jax import lax
from jax.experimental import pallas as pl
from jax.experimental.pallas import tpu as pltpu

N_HEADS = 8
HEAD_DIM = 128
N_LRU_BLOCKS = 16
LRU_C = 8.0
LRU_CONV = 4
FFN_CONV = 3
EPS = 1e-6

SUBLANES = 8
LRU_GROUP = 256
FFN_GROUP = 512
ATT_TILE = 256
VMEM_LIMIT = 56 * 1024 * 1024

BF16 = jnp.bfloat16
F32 = jnp.float32


def _dot(a, b):
    return jnp.dot(a, b, preferred_element_type=F32)


def _rms(x, g):
    return x * lax.rsqrt(jnp.mean(x * x, axis=-1, keepdims=True) + EPS) * g


def _gelu(x):
    return 0.5 * x * (1.0 + jnp.tanh(0.7978845608028654 * (x + 0.044715 * (x * x * x))))


def _sigmoid(x):
    return 0.5 * (jnp.tanh(0.5 * x) + 1.0)


def _log_sigmoid(x):
    return jnp.minimum(x, 0.0) - jnp.log(1.0 + jnp.exp(-jnp.abs(x)))


def _const_spec(shape):
    nd = len(shape)
    return pl.BlockSpec(shape, lambda *_: (0,) * nd, pipeline_mode=pl.Buffered(1))


def _params(sem):
    return pltpu.CompilerParams(dimension_semantics=sem, vmem_limit_bytes=VMEM_LIMIT)


def _carried_conv(cur, tail_ref, stage_ref, w_ref, b_ref, lo, width, n_streams, t_len):
    w_grp = cur.shape[-1]
    first = SUBLANES - (width - 1)
    outs = []
    for g in range(n_streams):
        stage_ref[g, 0:SUBLANES, :] = tail_ref[g, :, lo:lo + w_grp]
        stage_ref[g, SUBLANES:SUBLANES + t_len, :] = cur[g * t_len:(g + 1) * t_len]
        y = b_ref[:, lo:lo + w_grp]
        for k in range(width):
            y = y + w_ref[k:k + 1, lo:lo + w_grp] * stage_ref[g, first + k:first + k + t_len, :]
        tail_ref[g, :, lo:lo + w_grp] = stage_ref[g, t_len:t_len + SUBLANES, :]
        outs.append(y)
    return outs[0] if n_streams == 1 else jnp.concatenate(outs, axis=0)


def _rglru_kernel(x_ref, h0_ref, c0_ref, norm_ref, win_ref, cw_ref, cb_ref, wg_ref, br_ref, bi_ref,
                  lam_ref, wout_ref, y_ref, hT_ref, cT_ref, h_s, tail_s, stage_s, a_s, b_s,
                  *, n_streams, t_len):
    t = pl.program_id(1)
    n_t = pl.num_programs(1)
    width = x_ref.shape[-1]
    rows = n_streams * t_len
    n_grp = t_len // SUBLANES

    @pl.when(t == 0)
    def _():
        h_s[...] = h0_ref[...]
        tail_s[:, SUBLANES - (LRU_CONV - 1):SUBLANES, :] = c0_ref[...]

    x = x_ref[...].reshape(rows, width)
    xn = _rms(x, norm_ref[...]).astype(BF16)
    sub = lax.broadcasted_iota(jnp.int32, (n_grp, SUBLANES, LRU_GROUP), 1)

    acc = jnp.zeros((rows, width), F32)
    for cg in range(width // LRU_GROUP):
        lo = cg * LRU_GROUP
        gate = _dot(xn, win_ref[:, lo:lo + LRU_GROUP])
        rec = _dot(xn, win_ref[:, width + lo:width + lo + LRU_GROUP])
        c = _carried_conv(rec, tail_s, stage_s, cw_ref, cb_ref, lo, LRU_CONV, n_streams, t_len)
        ri = _dot(c.astype(BF16), wg_ref[cg])
        r = _sigmoid(ri[:, :LRU_GROUP] + br_ref[:, lo:lo + LRU_GROUP])
        i = _sigmoid(ri[:, LRU_GROUP:] + bi_ref[:, lo:lo + LRU_GROUP])
        log_a = (LRU_C * r) * _log_sigmoid(lam_ref[:, lo:lo + LRU_GROUP])
        a = jnp.exp(log_a)
        th = jnp.tanh(log_a)
        bt = jnp.sqrt(-2.0 * th / (1.0 - th)) * i * c

        hs = []
        for g in range(n_streams):
            ag = a[g * t_len:(g + 1) * t_len].reshape(n_grp, SUBLANES, LRU_GROUP)
            bg = bt[g * t_len:(g + 1) * t_len].reshape(n_grp, SUBLANES, LRU_GROUP)
            for d in (1, 2, 4):
                keep = sub >= d
                a_sh = pltpu.roll(ag, d, 1)
                b_sh = pltpu.roll(bg, d, 1)
                bg = jnp.where(keep, ag * b_sh + bg, bg)
                ag = jnp.where(keep, ag * a_sh, ag)
            a_s[...] = ag
            b_s[...] = bg

            def group_step(k, h_prev):
                h = a_s[k] * h_prev + b_s[k]
                a_s[k] = h
                return h[SUBLANES - 1:SUBLANES, :]

            h_last = lax.fori_loop(0, n_grp, group_step, h_s[g, :, lo:lo + LRU_GROUP])
            h_s[g, :, lo:lo + LRU_GROUP] = h_last
            hs.append(a_s[...].reshape(t_len, LRU_GROUP))
        h = hs[0] if n_streams == 1 else jnp.concatenate(hs, axis=0)
        acc = acc + _dot((h * _gelu(gate)).astype(BF16), wout_ref[lo:lo + LRU_GROUP, :])

    y_ref[...] = (x + acc).reshape(y_ref.shape)

    @pl.when(t == n_t - 1)
    def _():
        hT_ref[...] = h_s[...]
        cT_ref[...] = tail_s[:, SUBLANES - (LRU_CONV - 1):SUBLANES, :]


def _rglru(x, h0, c0, norm, w_in, conv_w, conv_b, w_gate, br, bi, lam, w_out, *, n_streams, t_len):
    bsz, seq, width = x.shape
    grid = (bsz // n_streams, seq // t_len)
    blk = lambda b, t: (b, t, 0)
    per_b = lambda b, t: (b, 0, 0)
    kern = functools.partial(_rglru_kernel, n_streams=n_streams, t_len=t_len)
    return pl.pallas_call(
        kern,
        grid=grid,
        in_specs=[
            pl.BlockSpec((n_streams, t_len, width), blk),
            pl.BlockSpec((n_streams, 1, width), per_b),
            pl.BlockSpec((n_streams, LRU_CONV - 1, width), per_b),
            _const_spec(norm.shape), _const_spec(w_in.shape), _const_spec(conv_w.shape),
            _const_spec(conv_b.shape), _const_spec(w_gate.shape), _const_spec(br.shape),
            _const_spec(bi.shape), _const_spec(lam.shape), _const_spec(w_out.shape),
        ],
        out_specs=[
            pl.BlockSpec((n_streams, t_len, width), blk),
            pl.BlockSpec((n_streams, 1, width), per_b),
            pl.BlockSpec((n_streams, LRU_CONV - 1, width), per_b),
        ],
        out_shape=[
            jax.ShapeDtypeStruct((bsz, seq, width), F32),
            jax.ShapeDtypeStruct((bsz, 1, width), F32),
            jax.ShapeDtypeStruct((bsz, LRU_CONV - 1, width), F32),
        ],
        scratch_shapes=[
            pltpu.VMEM((n_streams, 1, width), F32),
            pltpu.VMEM((n_streams, SUBLANES, width), F32),
            pltpu.VMEM((n_streams, SUBLANES + t_len, LRU_GROUP), F32),
            pltpu.VMEM((t_len // SUBLANES, SUBLANES, LRU_GROUP), F32),
            pltpu.VMEM((t_len // SUBLANES, SUBLANES, LRU_GROUP), F32),
        ],
        compiler_params=_params(("arbitrary", "arbitrary")),
        name="rglru",
    )(x, h0, c0, norm, w_in, conv_w, conv_b, w_gate, br, bi, lam, w_out)


def _ffn_kernel(*refs, n_streams, t_len, has_attn, final_norm):
    refs = list(refs)
    x_ref = refs.pop(0)
    o_ref = refs.pop(0) if has_attn else None
    wo_ref = refs.pop(0) if has_attn else None
    f0_ref, norm_ref, wup_ref, cw_ref, cb_ref, wdown_ref = refs[:6]
    refs = refs[6:]
    onorm_ref = refs.pop(0) if final_norm else None
    y_ref, fT_ref, tail_s, stage_s = refs

    t = pl.program_id(1)
    n_t = pl.num_programs(1)
    width = x_ref.shape[-1]
    d_ff = wdown_ref.shape[0]
    rows = n_streams * t_len

    @pl.when(t == 0)
    def _():
        tail_s[:, SUBLANES - (FFN_CONV - 1):SUBLANES, :] = f0_ref[...]

    x = x_ref[...].reshape(rows, width)
    if has_attn:
        x = x + _dot(o_ref[...].reshape(rows, width), wo_ref[...])
    xn = _rms(x, norm_ref[...]).astype(BF16)

    acc = jnp.zeros((rows, width), F32)
    for fg in range(d_ff // FFN_GROUP):
        lo = fg * FFN_GROUP
        gpre = _dot(xn, wup_ref[:, lo:lo + FFN_GROUP])
        u = _dot(xn, wup_ref[:, d_ff + lo:d_ff + lo + FFN_GROUP])
        gc = _carried_conv(gpre, tail_s, stage_s, cw_ref, cb_ref, lo, FFN_CONV, n_streams, t_len)
        acc = acc + _dot((_gelu(gc) * u).astype(BF16), wdown_ref[lo:lo + FFN_GROUP, :])

    y = x + acc
    if final_norm:
        y = _rms(y, onorm_ref[...])
    y_ref[...] = y.reshape(y_ref.shape)

    @pl.when(t == n_t - 1)
    def _():
        fT_ref[...] = tail_s[:, SUBLANES - (FFN_CONV - 1):SUBLANES, :]


def _ffn(x, f0, norm, w_up, conv_w, conv_b, w_down, *, n_streams, t_len, attn=None, out_norm=None):
    bsz, seq, width = x.shape
    d_ff = w_down.shape[0]
    grid = (bsz // n_streams, seq // t_len)
    blk = lambda b, t: (b, t, 0)
    per_b = lambda b, t: (b, 0, 0)
    args = [x]
    in_specs = [pl.BlockSpec((n_streams, t_len, width), blk)]
    if attn is not None:
        o, wo = attn
        args += [o, wo]
        in_specs += [pl.BlockSpec((n_streams, t_len, width), blk), _const_spec(wo.shape)]
    args += [f0, norm, w_up, conv_w, conv_b, w_down]
    in_specs += [pl.BlockSpec((n_streams, FFN_CONV - 1, d_ff), per_b), _const_spec(norm.shape),
                 _const_spec(w_up.shape), _const_spec(conv_w.shape), _const_spec(conv_b.shape),
                 _const_spec(w_down.shape)]
    if out_norm is not None:
        args.append(out_norm)
        in_specs.append(_const_spec(out_norm.shape))
    kern = functools.partial(_ffn_kernel, n_streams=n_streams, t_len=t_len,
                             has_attn=attn is not None, final_norm=out_norm is not None)
    return pl.pallas_call(
        kern,
        grid=grid,
        in_specs=in_specs,
        out_specs=[
            pl.BlockSpec((n_streams, t_len, width), blk),
            pl.BlockSpec((n_streams, FFN_CONV - 1, d_ff), per_b),
        ],
        out_shape=[
            jax.ShapeDtypeStruct((bsz, seq, width), F32),
            jax.ShapeDtypeStruct((bsz, FFN_CONV - 1, d_ff), F32),
        ],
        scratch_shapes=[
            pltpu.VMEM((n_streams, SUBLANES, d_ff), F32),
            pltpu.VMEM((n_streams, SUBLANES + t_len, FFN_GROUP), F32),
        ],
        compiler_params=_params(("arbitrary", "arbitrary")),
        name="ffn_attn" if attn is not None else "ffn",
    )(*args)


def _head_rms(x, g):
    outs = []
    for h in range(N_HEADS):
        outs.append(_rms(x[:, h * HEAD_DIM:(h + 1) * HEAD_DIM], g))
    return jnp.concatenate(outs, axis=-1)


def _qkv_kernel(x_ref, kvn_ref, wkv_ref, kn_ref, bn_ref, wq_ref, qn_ref,
                k_ref, v_ref, kb_ref, vb_ref, qb_ref):
    width = x_ref.shape[-1]
    x = x_ref[...]
    kv = _dot(_rms(x, kvn_ref[...]).astype(BF16), wkv_ref[...])
    k = _head_rms(kv[:, :width], kn_ref[...])
    v = kv[:, width:]
    k_ref[...] = k
    v_ref[...] = v
    kb_ref[...] = k.astype(BF16)
    vb_ref[...] = v.astype(BF16)
    q = _dot(_rms(x, bn_ref[...]).astype(BF16), wq_ref[...])
    qb_ref[...] = _head_rms(q, qn_ref[...]).astype(BF16)


def _qkv(x2d, kv_norm, w_kv, k_norm, b_norm, wq, q_norm, *, rows):
    n, width = x2d.shape
    blk = pl.BlockSpec((rows, width), lambda i: (i, 0))
    return pl.pallas_call(
        _qkv_kernel,
        grid=(n // rows,),
        in_specs=[blk, _const_spec(kv_norm.shape), _const_spec(w_kv.shape), _const_spec(k_norm.shape),
                  _const_spec(b_norm.shape), _const_spec(wq.shape), _const_spec(q_norm.shape)],
        out_specs=[blk] * 5,
        out_shape=[jax.ShapeDtypeStruct((n, width), F32)] * 2
        + [jax.ShapeDtypeStruct((n, width), BF16)] * 3,
        compiler_params=_params(("arbitrary",)),
        name="qkv",
    )(x2d, kv_norm, w_kv, k_norm, b_norm, wq, q_norm)


def _later_keys(n):
    j = lax.broadcasted_iota(jnp.int32, (n, n), 0)
    s = lax.broadcasted_iota(jnp.int32, (n, n), 1)
    return (j > s).astype(BF16)


def _sb_tile(q, kt, vt, later, carry, acc, mask):
    z = lax.dot_general(q, kt, (((1,), (1,)), ((), ())), preferred_element_type=F32)
    z = z * (HEAD_DIM ** -0.5)
    log_beta = _log_sigmoid(z)
    log_stay = log_beta - z
    if mask is not None:
        log_stay = jnp.where(mask, log_stay, 0.0)
    hi = log_stay.astype(BF16)
    lo = (log_stay - hi.astype(F32)).astype(BF16)
    log_after = _dot(hi, later) + _dot(lo, later) + carry
    w = jnp.exp(log_beta + log_after)
    if mask is not None:
        w = jnp.where(mask, w, 0.0)
    acc = acc + _dot(w.astype(BF16), vt)
    carry = carry + jnp.sum(log_stay, axis=-1, keepdims=True)
    return carry, acc


def _attn_prompt_kernel(q_ref, k_ref, v_ref, o_ref):
    i = pl.program_id(2)
    tq = q_ref.shape[1]
    q = q_ref[0]
    later = _later_keys(tq)
    row = lax.broadcasted_iota(jnp.int32, (tq, tq), 0)
    col = lax.broadcasted_iota(jnp.int32, (tq, tq), 1)

    def tile(j, carry, acc, mask):
        start = pl.multiple_of(j * tq, tq)
        return _sb_tile(q, k_ref[0, pl.ds(start, tq), :], v_ref[0, pl.ds(start, tq), :],
                        later, carry, acc, mask)

    carry = jnp.zeros((tq, 1), F32)
    acc = jnp.zeros((tq, HEAD_DIM), F32)
    carry, acc = tile(i, carry, acc, col < row)
    carry, acc = lax.fori_loop(0, i, lambda n, ca: tile(i - 1 - n, ca[0], ca[1], None), (carry, acc))
    o_ref[0] = acc.astype(o_ref.dtype)


def _attn_prompt(qb, kb, vb):
    bsz, seq, width = qb.shape
    tq = min(ATT_TILE, seq)
    grid = (bsz, N_HEADS, seq // tq)
    q_spec = pl.BlockSpec((1, tq, HEAD_DIM), lambda b, h, i: (b, i, h))
    kv_spec = pl.BlockSpec((1, seq, HEAD_DIM), lambda b, h, i: (b, 0, h))
    return pl.pallas_call(
        _attn_prompt_kernel,
        grid=grid,
        in_specs=[q_spec, kv_spec, kv_spec],
        out_specs=q_spec,
        out_shape=jax.ShapeDtypeStruct((bsz, seq, width), BF16),
        compiler_params=_params(("arbitrary", "arbitrary", "arbitrary")),
        name="attn_prompt",
    )(qb, kb, vb)


def _attn_sample_kernel(qbd_ref, kn_ref, vn_ref, kc_ref, vc_ref, o_ref, carry_s, acc_s, *, t_q):
    j = pl.program_id(1)
    n_j = pl.num_programs(1)
    rows = qbd_ref.shape[1]
    q = qbd_ref[0]

    @pl.when(j == 0)
    def _():
        n_new = kn_ref.shape[1]
        t = lax.broadcasted_iota(jnp.int32, (rows, n_new), 0) % t_q
        s = lax.broadcasted_iota(jnp.int32, (rows, n_new), 1)
        carry, acc = _sb_tile(q, kn_ref[0], vn_ref[0], _later_keys(n_new),
                              jnp.zeros((rows, 1), F32), jnp.zeros(acc_s.shape, F32), s < t)
        carry_s[...] = carry
        acc_s[...] = acc

    @pl.when(j > 0)
    def _():
        n_tiles = kc_ref.shape[1] // ATT_TILE
        later = _later_keys(ATT_TILE)

        def tile(n, ca):
            start = pl.multiple_of((n_tiles - 1 - n) * ATT_TILE, ATT_TILE)
            kt = kc_ref[0, pl.ds(start, ATT_TILE), :].astype(BF16)
            vt = vc_ref[0, pl.ds(start, ATT_TILE), :].astype(BF16)
            return _sb_tile(q, kt, vt, later, ca[0], ca[1], None)

        carry, acc = lax.fori_loop(0, n_tiles, tile, (carry_s[...], acc_s[...]))
        carry_s[...] = carry
        acc_s[...] = acc

    @pl.when(j == n_j - 1)
    def _():
        for h in range(N_HEADS):
            o_ref[0, :, h * HEAD_DIM:(h + 1) * HEAD_DIM] = acc_s[
                h * t_q:(h + 1) * t_q, h * HEAD_DIM:(h + 1) * HEAD_DIM].astype(o_ref.dtype)


def _attn_sample(qbd, k_new, v_new, cache_k, cache_v, *, t_q, chunk):
    bsz, rows, width = qbd.shape
    past = cache_k.shape[1]
    n_chunks = past // chunk
    n_new = k_new.shape[1]
    cache_spec = pl.BlockSpec((1, chunk, width), lambda b, j: (b, jnp.minimum(n_chunks - j, n_chunks - 1), 0))
    new_spec = pl.BlockSpec((1, n_new, width), lambda b, j: (b, 0, 0))
    kern = functools.partial(_attn_sample_kernel, t_q=t_q)
    return pl.pallas_call(
        kern,
        grid=(bsz, n_chunks + 1),
        in_specs=[pl.BlockSpec((1, rows, width), lambda b, j: (b, 0, 0)), new_spec, new_spec,
                  cache_spec, cache_spec],
        out_specs=pl.BlockSpec((1, t_q, width), lambda b, j: (b, 0, 0)),
        out_shape=jax.ShapeDtypeStruct((bsz, t_q, width), BF16),
        scratch_shapes=[pltpu.VMEM((rows, 1), F32), pltpu.VMEM((rows, width), F32)],
        compiler_params=_params(("arbitrary", "arbitrary")),
        name="attn_sample",
    )(qbd, k_new, v_new, cache_k, cache_v)


def _gate_weights(wr, wi):
    per = LRU_GROUP // wr.shape[-1]
    n_grp = wr.shape[0] // per
    eye = jnp.eye(per, dtype=wr.dtype)

    def expand(w):
        w = w.reshape(n_grp, per, w.shape[1], w.shape[2])
        return jnp.einsum('gnde,nm->gndme', w, eye).reshape(n_grp, LRU_GROUP, LRU_GROUP)

    return jnp.concatenate([expand(wr), expand(wi)], axis=-1).astype(BF16)


def _block_diag_queries(qb, t_q):
    bsz = qb.shape[0]
    q = qb.reshape(bsz, t_q, N_HEADS, HEAD_DIM).transpose(0, 2, 1, 3)
    eye = jnp.eye(N_HEADS, dtype=qb.dtype)
    return jnp.einsum('bhtd,hg->bhtgd', q, eye).reshape(bsz, N_HEADS * t_q, N_HEADS * HEAD_DIM)


def _trunk(x, lru_h, lru_conv, ffn_conv, cache, w, *, n_streams, t_len, qkv_rows):
    bsz, seq, width = x.shape
    row = lambda v: v.reshape(1, -1)
    x1, h_new, c_new = _rglru(
        x, lru_h[0][:, None, :], lru_conv[0], row(w['a_norm'][0]), w['a_w_in'][0], w['a_conv_w'][0],
        row(w['a_conv_b'][0]), w['a_gate'], row(w['a_br'][0]), row(w['a_bi'][0]), row(w['a_lambda'][0]),
        w['a_w_out'][0], n_streams=n_streams, t_len=t_len)
    x2, f0_new = _ffn(x1, ffn_conv[0], row(w['f_norm'][0]), w['f_w_up'][0], w['f_conv_w'][0],
                      row(w['f_conv_b'][0]), w['f_w_down'][0], n_streams=n_streams, t_len=t_len)
    k, v, kb, vb, qb = _qkv(x2.reshape(bsz * seq, width), row(w['kv_norm']), w['w_kv'], row(w['k_norm']),
                            row(w['b_norm'][0]), w['b_wq'][0], row(w['q_norm'][0]), rows=qkv_rows)
    shape3 = (bsz, seq, width)
    kb, vb, qb = kb.reshape(shape3), vb.reshape(shape3), qb.reshape(shape3)
    if cache is None:
        o = _attn_prompt(qb, kb, vb)
    else:
        pad = ((0, 0), (0, HEAD_DIM - seq), (0, 0))
        o = _attn_sample(_block_diag_queries(qb, seq), jnp.pad(kb, pad), jnp.pad(vb, pad),
                         cache[0], cache[1], t_q=seq, chunk=1024)
    y, f1_new = _ffn(x2, ffn_conv[1], row(w['f_norm'][1]), w['f_w_up'][1], w['f_conv_w'][1],
                     row(w['f_conv_b'][1]), w['f_w_down'][1], n_streams=n_streams, t_len=t_len,
                     attn=(o, w['b_wo'][0]), out_norm=row(w['out_norm']))
    heads = (bsz, seq, N_HEADS, HEAD_DIM)
    return (y, h_new.reshape(1, bsz, width), c_new[None], jnp.stack([f0_new, f1_new]),
            k.reshape(heads), v.reshape(heads))


def kernel(x_prompt, x_sample, state_lru_h, state_lru_conv, state_ffn_conv, cache_k, cache_v, a_norm, a_w_in, a_conv_w, a_conv_b, a_wr, a_br, a_wi, a_bi, a_lambda, a_w_out, kv_norm, w_kv, k_norm, b_norm, b_wq, q_norm, b_wo, f_norm, f_w_up, f_conv_w, f_conv_b, f_w_down, out_norm):
    w = dict(a_norm=a_norm, a_w_in=a_w_in.astype(BF16), a_conv_w=a_conv_w, a_conv_b=a_conv_b,
             a_gate=_gate_weights(a_wr[0], a_wi[0]), a_br=a_br, a_bi=a_bi, a_lambda=a_lambda,
             a_w_out=a_w_out.astype(BF16), kv_norm=kv_norm, w_kv=w_kv.astype(BF16), k_norm=k_norm,
             b_norm=b_norm, b_wq=b_wq.astype(BF16), q_norm=q_norm, b_wo=b_wo.astype(BF16),
             f_norm=f_norm, f_w_up=f_w_up.astype(BF16), f_conv_w=f_conv_w, f_conv_b=f_conv_b,
             f_w_down=f_w_down.astype(BF16), out_norm=out_norm)
    bp, sp, width = x_prompt.shape
    d_ff = f_w_down.shape[1]
    n_lru = a_w_in.shape[0]
    depth = f_w_up.shape[0]
    zeros = lambda *s: jnp.zeros(s, x_prompt.dtype)
    prompt = _trunk(x_prompt, zeros(n_lru, bp, width), zeros(n_lru, bp, LRU_CONV - 1, width),
                    zeros(depth, bp, FFN_CONV - 1, d_ff), None, w,
                    n_streams=1, t_len=min(512, sp), qkv_rows=min(512, bp * sp))
    bs, ss, _ = x_sample.shape
    sample = _trunk(x_sample, state_lru_h, state_lru_conv, state_ffn_conv, (cache_k.reshape(bs, -1, width),
                    cache_v.reshape(bs, -1, width)), w, n_streams=bs, t_len=ss, qkv_rows=bs * ss)
    y_p, h_p, c_p, f_p, k_p, v_p = prompt
    y_s, h_s, c_s, f_s, k_s, v_s = sample
    return (y_p, y_s, h_p, c_p, f_p, k_p, v_p, h_s, c_s, f_s, k_s, v_s)
```

```python
import functools
import math

import jax
import jax.numpy as jnp
from jax import lax
from jax.experimental import pallas as pl
from jax.experimental.pallas import tpu as pltpu

N_HEADS = 8
HEAD_DIM = 128
N_LRU_BLOCKS = 16
LRU_C = 8.0
LRU_CONV = 4
FFN_CONV = 3
EPS = 1e-6

SUBLANES = 8
LRU_GROUP = 256
FFN_GROUP = 512
ATT_TILE = 256
ATT_HEADS = 8
ATT_SKEW = 2
VMEM_LIMIT = 56 * 1024 * 1024

BF16 = jnp.bfloat16
F32 = jnp.float32


def _dot(a, b):
    return jnp.dot(a, b, preferred_element_type=F32)


def _rms(x, g):
    return x * lax.rsqrt(jnp.mean(x * x, axis=-1, keepdims=True) + EPS) * g


def _gelu(x):
    return 0.5 * x * (1.0 + jnp.tanh(0.7978845608028654 * (x + 0.044715 * (x * x * x))))


def _sigmoid(x):
    return 0.5 * (jnp.tanh(0.5 * x) + 1.0)


def _log_sigmoid(x):
    return jnp.minimum(x, 0.0) - jnp.log(1.0 + jnp.exp(-jnp.abs(x)))


def _const_spec(shape):
    nd = len(shape)
    return pl.BlockSpec(shape, lambda *_: (0,) * nd, pipeline_mode=pl.Buffered(1))


def _params(sem):
    return pltpu.CompilerParams(dimension_semantics=sem, vmem_limit_bytes=VMEM_LIMIT)


def _carried_conv(cur, tail_ref, stage_ref, w_ref, b_ref, lo, width, n_streams, t_len):
    w_grp = cur.shape[-1]
    first = SUBLANES - (width - 1)
    outs = []
    for g in range(n_streams):
        stage_ref[g, 0:SUBLANES, :] = tail_ref[g, :, lo:lo + w_grp]
        stage_ref[g, SUBLANES:SUBLANES + t_len, :] = cur[g * t_len:(g + 1) * t_len]
        y = b_ref[:, lo:lo + w_grp]
        for k in range(width):
            y = y + w_ref[k:k + 1, lo:lo + w_grp] * stage_ref[g, first + k:first + k + t_len, :]
        tail_ref[g, :, lo:lo + w_grp] = stage_ref[g, t_len:t_len + SUBLANES, :]
        outs.append(y)
    return outs[0] if n_streams == 1 else jnp.concatenate(outs, axis=0)


def _rglru_kernel(x_ref, h0_ref, c0_ref, norm_ref, win_ref, cw_ref, cb_ref, wg_ref, br_ref, bi_ref,
                  lam_ref, wout_ref, y_ref, hT_ref, cT_ref, h_s, tail_s, stage_s, a_s, b_s,
                  *, n_streams, t_len):
    t = pl.program_id(1)
    n_t = pl.num_programs(1)
    width = x_ref.shape[-1]
    rows = n_streams * t_len
    n_grp = t_len // SUBLANES

    @pl.when(t == 0)
    def _():
        h_s[...] = h0_ref[...]
        tail_s[:, SUBLANES - (LRU_CONV - 1):SUBLANES, :] = c0_ref[...]

    x = x_ref[...].reshape(rows, width)
    xn = _rms(x, norm_ref[...]).astype(BF16)
    sub = lax.broadcasted_iota(jnp.int32, (n_grp, SUBLANES, LRU_GROUP), 1)

    acc = jnp.zeros((rows, width), F32)
    for cg in range(width // LRU_GROUP):
        lo = cg * LRU_GROUP
        gate = _dot(xn, win_ref[:, lo:lo + LRU_GROUP])
        rec = _dot(xn, win_ref[:, width + lo:width + lo + LRU_GROUP])
        c = _carried_conv(rec, tail_s, stage_s, cw_ref, cb_ref, lo, LRU_CONV, n_streams, t_len)
        ri = _dot(c.astype(BF16), wg_ref[cg])
        r = _sigmoid(ri[:, :LRU_GROUP] + br_ref[:, lo:lo + LRU_GROUP])
        i = _sigmoid(ri[:, LRU_GROUP:] + bi_ref[:, lo:lo + LRU_GROUP])
        log_a = (LRU_C * r) * _log_sigmoid(lam_ref[:, lo:lo + LRU_GROUP])
        a = jnp.exp(log_a)
        th = jnp.tanh(log_a)
        bt = jnp.sqrt(-2.0 * th / (1.0 - th)) * i * c

        hs = []
        for g in range(n_streams):
            ag = a[g * t_len:(g + 1) * t_len].reshape(n_grp, SUBLANES, LRU_GROUP)
            bg = bt[g * t_len:(g + 1) * t_len].reshape(n_grp, SUBLANES, LRU_GROUP)
            for d in (1, 2, 4):
                keep = sub >= d
                a_sh = pltpu.roll(ag, d, 1)
                b_sh = pltpu.roll(bg, d, 1)
                bg = jnp.where(keep, ag * b_sh + bg, bg)
                ag = jnp.where(keep, ag * a_sh, ag)
            a_s[...] = ag
            b_s[...] = bg

            def group_step(k, h_prev):
                h = a_s[k] * h_prev + b_s[k]
                a_s[k] = h
                return h[SUBLANES - 1:SUBLANES, :]

            h_last = lax.fori_loop(0, n_grp, group_step, h_s[g, :, lo:lo + LRU_GROUP])
            h_s[g, :, lo:lo + LRU_GROUP] = h_last
            hs.append(a_s[...].reshape(t_len, LRU_GROUP))
        h = hs[0] if n_streams == 1 else jnp.concatenate(hs, axis=0)
        acc = acc + _dot((h * _gelu(gate)).astype(BF16), wout_ref[lo:lo + LRU_GROUP, :])

    y_ref[...] = (x + acc).reshape(y_ref.shape)

    @pl.when(t == n_t - 1)
    def _():
        hT_ref[...] = h_s[...]
        cT_ref[...] = tail_s[:, SUBLANES - (LRU_CONV - 1):SUBLANES, :]


def _rglru(x, h0, c0, norm, w_in, conv_w, conv_b, w_gate, br, bi, lam, w_out, *, n_streams, t_len):
    bsz, seq, width = x.shape
    grid = (bsz // n_streams, seq // t_len)
    blk = lambda b, t: (b, t, 0)
    per_b = lambda b, t: (b, 0, 0)
    kern = functools.partial(_rglru_kernel, n_streams=n_streams, t_len=t_len)
    return pl.pallas_call(
        kern,
        grid=grid,
        in_specs=[
            pl.BlockSpec((n_streams, t_len, width), blk),
            pl.BlockSpec((n_streams, 1, width), per_b),
            pl.BlockSpec((n_streams, LRU_CONV - 1, width), per_b),
            _const_spec(norm.shape), _const_spec(w_in.shape), _const_spec(conv_w.shape),
            _const_spec(conv_b.shape), _const_spec(w_gate.shape), _const_spec(br.shape),
            _const_spec(bi.shape), _const_spec(lam.shape), _const_spec(w_out.shape),
        ],
        out_specs=[
            pl.BlockSpec((n_streams, t_len, width), blk),
            pl.BlockSpec((n_streams, 1, width), per_b),
            pl.BlockSpec((n_streams, LRU_CONV - 1, width), per_b),
        ],
        out_shape=[
            jax.ShapeDtypeStruct((bsz, seq, width), F32),
            jax.ShapeDtypeStruct((bsz, 1, width), F32),
            jax.ShapeDtypeStruct((bsz, LRU_CONV - 1, width), F32),
        ],
        scratch_shapes=[
            pltpu.VMEM((n_streams, 1, width), F32),
            pltpu.VMEM((n_streams, SUBLANES, width), F32),
            pltpu.VMEM((n_streams, SUBLANES + t_len, LRU_GROUP), F32),
            pltpu.VMEM((t_len // SUBLANES, SUBLANES, LRU_GROUP), F32),
            pltpu.VMEM((t_len // SUBLANES, SUBLANES, LRU_GROUP), F32),
        ],
        compiler_params=_params(("arbitrary", "arbitrary")),
        name="rglru",
    )(x, h0, c0, norm, w_in, conv_w, conv_b, w_gate, br, bi, lam, w_out)


def _ffn_kernel(*refs, n_streams, t_len, has_attn, final_norm):
    refs = list(refs)
    x_ref = refs.pop(0)
    o_ref = refs.pop(0) if has_attn else None
    wo_ref = refs.pop(0) if has_attn else None
    f0_ref, norm_ref, wup_ref, cw_ref, cb_ref, wdown_ref = refs[:6]
    refs = refs[6:]
    onorm_ref = refs.pop(0) if final_norm else None
    y_ref, fT_ref, tail_s, stage_s = refs

    t = pl.program_id(1)
    n_t = pl.num_programs(1)
    width = x_ref.shape[-1]
    d_ff = wdown_ref.shape[0]
    rows = n_streams * t_len

    @pl.when(t == 0)
    def _():
        tail_s[:, SUBLANES - (FFN_CONV - 1):SUBLANES, :] = f0_ref[...]

    x = x_ref[...].reshape(rows, width)
    if has_attn:
        x = x + _dot(o_ref[...].reshape(rows, width), wo_ref[...])
    xn = _rms(x, norm_ref[...]).astype(BF16)

    acc = jnp.zeros((rows, width), F32)
    for fg in range(d_ff // FFN_GROUP):
        lo = fg * FFN_GROUP
        gpre = _dot(xn, wup_ref[:, lo:lo + FFN_GROUP])
        u = _dot(xn, wup_ref[:, d_ff + lo:d_ff + lo + FFN_GROUP])
        gc = _carried_conv(gpre, tail_s, stage_s, cw_ref, cb_ref, lo, FFN_CONV, n_streams, t_len)
        acc = acc + _dot((_gelu(gc) * u).astype(BF16), wdown_ref[lo:lo + FFN_GROUP, :])

    y = x + acc
    if final_norm:
        y = _rms(y, onorm_ref[...])
    y_ref[...] = y.reshape(y_ref.shape)

    @pl.when(t == n_t - 1)
    def _():
        fT_ref[...] = tail_s[:, SUBLANES - (FFN_CONV - 1):SUBLANES, :]


def _ffn(x, f0, norm, w_up, conv_w, conv_b, w_down, *, n_streams, t_len, attn=None, out_norm=None):
    bsz, seq, width = x.shape
    d_ff = w_down.shape[0]
    grid = (bsz // n_streams, seq // t_len)
    blk = lambda b, t: (b, t, 0)
    per_b = lambda b, t: (b, 0, 0)
    args = [x]
    in_specs = [pl.BlockSpec((n_streams, t_len, width), blk)]
    if attn is not None:
        o, wo = attn
        args += [o, wo]
        in_specs += [pl.BlockSpec((n_streams, t_len, width), blk), _const_spec(wo.shape)]
    args += [f0, norm, w_up, conv_w, conv_b, w_down]
    in_specs += [pl.BlockSpec((n_streams, FFN_CONV - 1, d_ff), per_b), _const_spec(norm.shape),
                 _const_spec(w_up.shape), _const_spec(conv_w.shape), _const_spec(conv_b.shape),
                 _const_spec(w_down.shape)]
    if out_norm is not None:
        args.append(out_norm)
        in_specs.append(_const_spec(out_norm.shape))
    kern = functools.partial(_ffn_kernel, n_streams=n_streams, t_len=t_len,
                             has_attn=attn is not None, final_norm=out_norm is not None)
    return pl.pallas_call(
        kern,
        grid=grid,
        in_specs=in_specs,
        out_specs=[
            pl.BlockSpec((n_streams, t_len, width), blk),
            pl.BlockSpec((n_streams, FFN_CONV - 1, d_ff), per_b),
        ],
        out_shape=[
            jax.ShapeDtypeStruct((bsz, seq, width), F32),
            jax.ShapeDtypeStruct((bsz, FFN_CONV - 1, d_ff), F32),
        ],
        scratch_shapes=[
            pltpu.VMEM((n_streams, SUBLANES, d_ff), F32),
            pltpu.VMEM((n_streams, SUBLANES + t_len, FFN_GROUP), F32),
        ],
        compiler_params=_params(("arbitrary", "arbitrary")),
        name="ffn_attn" if attn is not None else "ffn",
    )(*args)


def _head_rms(x, g):
    outs = []
    for h in range(N_HEADS):
        outs.append(_rms(x[:, h * HEAD_DIM:(h + 1) * HEAD_DIM], g))
    return jnp.concatenate(outs, axis=-1)


def _qkv_kernel(x_ref, kvn_ref, wkv_ref, kn_ref, bn_ref, wq_ref, qn_ref,
                k_ref, v_ref, kb_ref, vb_ref, qb_ref):
    width = x_ref.shape[-1]
    x = x_ref[...]
    kv = _dot(_rms(x, kvn_ref[...]).astype(BF16), wkv_ref[...])
    k = _head_rms(kv[:, :width], kn_ref[...])
    v = kv[:, width:]
    for h in range(N_HEADS):
        k_ref[:, h, :] = k[:, h * HEAD_DIM:(h + 1) * HEAD_DIM]
        v_ref[:, h, :] = v[:, h * HEAD_DIM:(h + 1) * HEAD_DIM]
    kb_ref[...] = k.astype(BF16)
    vb_ref[...] = v.astype(BF16)
    q = _dot(_rms(x, bn_ref[...]).astype(BF16), wq_ref[...])
    qb_ref[...] = _head_rms(q, qn_ref[...]).astype(BF16)


def _qkv(x2d, kv_norm, w_kv, k_norm, b_norm, wq, q_norm, *, rows):
    n, width = x2d.shape
    blk = pl.BlockSpec((rows, width), lambda i: (i, 0))
    return pl.pallas_call(
        _qkv_kernel,
        grid=(n // rows,),
        in_specs=[blk, _const_spec(kv_norm.shape), _const_spec(w_kv.shape), _const_spec(k_norm.shape),
                  _const_spec(b_norm.shape), _const_spec(wq.shape), _const_spec(q_norm.shape)],
        out_specs=[pl.BlockSpec((rows, N_HEADS, HEAD_DIM), lambda i: (i, 0, 0))] * 2 + [blk] * 3,
        out_shape=[jax.ShapeDtypeStruct((n, N_HEADS, HEAD_DIM), F32)] * 2
        + [jax.ShapeDtypeStruct((n, width), BF16)] * 3,
        compiler_params=_params(("arbitrary",)),
        name="qkv",
    )(x2d, kv_norm, w_kv, k_norm, b_norm, wq, q_norm)


def _later_keys(n):
    j = lax.broadcasted_iota(jnp.int32, (2 * n, n), 0) % n
    s = lax.broadcasted_iota(jnp.int32, (2 * n, n), 1)
    return (j > s).astype(BF16)


def _neg_abs(x):
    bits = lax.bitcast_convert_type(x, jnp.uint32) | jnp.uint32(0x80000000)
    return lax.bitcast_convert_type(bits, F32)


def _sb_tiles(qs, kts, vts, later2, state, mask):
    n = len(qs)
    log_betas, log_stays, splits, ws, out = {}, {}, {}, {}, {}
    for step in range(n + 2 * ATT_SKEW):
        s = step
        if s < n:
            z = lax.dot_general(qs[s], kts[s], (((1,), (1,)), ((), ())), preferred_element_type=F32)
            y = z * (HEAD_DIM ** -0.5)
            log_beta = jnp.minimum(y, 0.0) - jnp.log(1.0 + jnp.exp(_neg_abs(y)))
            log_stay = log_beta - y
            if mask is not None:
                log_stay = jnp.where(mask, log_stay, 0.0)
            hi = log_stay.astype(BF16)
            lo = (log_stay - hi.astype(F32)).astype(BF16)
            log_betas[s] = log_beta
            splits[s] = jnp.concatenate([hi, lo], axis=1)
            out[2 * s] = state[2 * s] + jnp.sum(log_stay, axis=-1, keepdims=True)
        s = step - ATT_SKEW
        if 0 <= s < n:
            w = jnp.exp(log_betas.pop(s) + _dot(splits.pop(s), later2) + state[2 * s])
            if mask is not None:
                w = jnp.where(mask, w, 0.0)
            ws[s] = w.astype(BF16)
        s = step - 2 * ATT_SKEW
        if 0 <= s < n:
            out[2 * s + 1] = state[2 * s + 1] + _dot(ws.pop(s), vts[s])
    return tuple(out[k] for k in range(2 * n))


def _attn_prompt_kernel(q_ref, k_ref, v_ref, o_ref):
    i = pl.program_id(2)
    tq = q_ref.shape[1]
    n_heads = q_ref.shape[2] // HEAD_DIM
    later = _later_keys(tq)
    row = lax.broadcasted_iota(jnp.int32, (tq, tq), 0)
    col = lax.broadcasted_iota(jnp.int32, (tq, tq), 1)
    heads = [slice(h * HEAD_DIM, (h + 1) * HEAD_DIM) for h in range(n_heads)]
    qs = [q_ref[0, :, hs] for hs in heads]

    def tile(j, state, mask):
        start = pl.multiple_of(j * tq, tq)
        return _sb_tiles(qs, [k_ref[0, pl.ds(start, tq), hs] for hs in heads],
                         [v_ref[0, pl.ds(start, tq), hs] for hs in heads], later, state, mask)

    state = (jnp.zeros((tq, 1), F32), jnp.zeros((tq, HEAD_DIM), F32)) * n_heads
    state = tile(i, state, col < row)
    state = lax.fori_loop(0, i, lambda n, st: tile(i - 1 - n, st, None), state)
    for h, hs in enumerate(heads):
        o_ref[0, :, hs] = state[2 * h + 1].astype(o_ref.dtype)


def _attn_prompt(qb, kb, vb):
    bsz, seq, width = qb.shape
    tq = min(ATT_TILE, seq)
    lanes = ATT_HEADS * HEAD_DIM
    grid = (bsz, width // lanes, seq // tq)
    q_spec = pl.BlockSpec((1, tq, lanes), lambda b, h, i: (b, i, h))
    kv_spec = pl.BlockSpec((1, seq, lanes), lambda b, h, i: (b, 0, h))
    return pl.pallas_call(
        _attn_prompt_kernel,
        grid=grid,
        in_specs=[q_spec, kv_spec, kv_spec],
        out_specs=q_spec,
        out_shape=jax.ShapeDtypeStruct((bsz, seq, width), BF16),
        compiler_params=_params(("arbitrary", "arbitrary", "arbitrary")),
        name="attn_prompt",
    )(qb, kb, vb)


def _attn_sample_kernel(qbd_ref, kn_ref, vn_ref, kc_ref, vc_ref, o_ref, carry_s, acc_s, *, t_q):
    j = pl.program_id(1)
    n_j = pl.num_programs(1)
    rows = qbd_ref.shape[1]
    q = qbd_ref[0]

    @pl.when(j == 0)
    def _():
        n_new = kn_ref.shape[1]
        t = lax.broadcasted_iota(jnp.int32, (rows, n_new), 0) % t_q
        s = lax.broadcasted_iota(jnp.int32, (rows, n_new), 1)
        carry, acc = _sb_tiles([q], [kn_ref[0]], [vn_ref[0]], _later_keys(n_new),
                               (jnp.zeros((rows, 1), F32), jnp.zeros(acc_s.shape, F32)), s < t)
        carry_s[...] = carry
        acc_s[...] = acc

    @pl.when(j > 0)
    def _():
        n_tiles = kc_ref.shape[1] // ATT_TILE
        later = _later_keys(ATT_TILE)

        def tile(n, ca):
            start = pl.multiple_of((n_tiles - 1 - n) * ATT_TILE, ATT_TILE)
            kt = jnp.concatenate([kc_ref[0, pl.ds(start, ATT_TILE), h, :].astype(BF16)
                                  for h in range(N_HEADS)], axis=1)
            vt = jnp.concatenate([vc_ref[0, pl.ds(start, ATT_TILE), h, :].astype(BF16)
                                  for h in range(N_HEADS)], axis=1)
            return _sb_tiles([q], [kt], [vt], later, ca, None)

        carry, acc = lax.fori_loop(0, n_tiles, tile, (carry_s[...], acc_s[...]))
        carry_s[...] = carry
        acc_s[...] = acc

    @pl.when(j == n_j - 1)
    def _():
        for h in range(N_HEADS):
            o_ref[0, :, h * HEAD_DIM:(h + 1) * HEAD_DIM] = acc_s[
                h * t_q:(h + 1) * t_q, h * HEAD_DIM:(h + 1) * HEAD_DIM].astype(o_ref.dtype)


def _attn_sample(qbd, k_new, v_new, cache_k, cache_v, *, t_q, chunk):
    bsz, rows, width = qbd.shape
    past = cache_k.shape[1]
    n_chunks = past // chunk
    n_new = k_new.shape[1]
    cache_spec = pl.BlockSpec((1, chunk, N_HEADS, HEAD_DIM),
                              lambda b, j: (b, jnp.minimum(n_chunks - j, n_chunks - 1), 0, 0))
    new_spec = pl.BlockSpec((1, n_new, width), lambda b, j: (b, 0, 0))
    kern = functools.partial(_attn_sample_kernel, t_q=t_q)
    return pl.pallas_call(
        kern,
        grid=(bsz, n_chunks + 1),
        in_specs=[pl.BlockSpec((1, rows, width), lambda b, j: (b, 0, 0)), new_spec, new_spec,
                  cache_spec, cache_spec],
        out_specs=pl.BlockSpec((1, t_q, width), lambda b, j: (b, 0, 0)),
        out_shape=jax.ShapeDtypeStruct((bsz, t_q, width), BF16),
        scratch_shapes=[pltpu.VMEM((rows, 1), F32), pltpu.VMEM((rows, width), F32)],
        compiler_params=_params(("arbitrary", "arbitrary")),
        name="attn_sample",
    )(qbd, k_new, v_new, cache_k, cache_v)


def _gate_weights(wr, wi):
    per = LRU_GROUP // wr.shape[-1]
    n_grp = wr.shape[0] // per
    eye = jnp.eye(per, dtype=wr.dtype)

    def expand(w):
        w = w.reshape(n_grp, per, w.shape[1], w.shape[2])
        return jnp.einsum('gnde,nm->gndme', w, eye).reshape(n_grp, LRU_GROUP, LRU_GROUP)

    return jnp.concatenate([expand(wr), expand(wi)], axis=-1).astype(BF16)


def _block_diag_queries(qb, t_q):
    bsz = qb.shape[0]
    q = qb.reshape(bsz, t_q, N_HEADS, HEAD_DIM).transpose(0, 2, 1, 3)
    eye = jnp.eye(N_HEADS, dtype=qb.dtype)
    return jnp.einsum('bhtd,hg->bhtgd', q, eye).reshape(bsz, N_HEADS * t_q, N_HEADS * HEAD_DIM)


def _trunk(x, lru_h, lru_conv, ffn_conv, cache, w, *, n_streams, t_len, qkv_rows):
    bsz, seq, width = x.shape
    row = lambda v: v.reshape(1, -1)
    x1, h_new, c_new = _rglru(
        x, lru_h[0][:, None, :], lru_conv[0], row(w['a_norm'][0]), w['a_w_in'][0], w['a_conv_w'][0],
        row(w['a_conv_b'][0]), w['a_gate'], row(w['a_br'][0]), row(w['a_bi'][0]), row(w['a_lambda'][0]),
        w['a_w_out'][0], n_streams=n_streams, t_len=t_len)
    x2, f0_new = _ffn(x1, ffn_conv[0], row(w['f_norm'][0]), w['f_w_up'][0], w['f_conv_w'][0],
                      row(w['f_conv_b'][0]), w['f_w_down'][0], n_streams=n_streams, t_len=t_len)
    k, v, kb, vb, qb = _qkv(x2.reshape(bsz * seq, width), row(w['kv_norm']), w['w_kv'], row(w['k_norm']),
                            row(w['b_norm'][0]), w['b_wq'][0], row(w['q_norm'][0]), rows=qkv_rows)
    shape3 = (bsz, seq, width)
    kb, vb, qb = kb.reshape(shape3), vb.reshape(shape3), qb.reshape(shape3)
    if cache is None:
        o = _attn_prompt(qb, kb, vb)
    else:
        pad = ((0, 0), (0, HEAD_DIM - seq), (0, 0))
        o = _attn_sample(_block_diag_queries(qb, seq), jnp.pad(kb, pad), jnp.pad(vb, pad),
                         cache[0], cache[1], t_q=seq, chunk=1024)
    y, f1_new = _ffn(x2, ffn_conv[1], row(w['f_norm'][1]), w['f_w_up'][1], w['f_conv_w'][1],
                     row(w['f_conv_b'][1]), w['f_w_down'][1], n_streams=n_streams, t_len=t_len,
                     attn=(o, w['b_wo'][0]), out_norm=row(w['out_norm']))
    heads = (bsz, seq, N_HEADS, HEAD_DIM)
    return (y, h_new.reshape(1, bsz, width), c_new[None], jnp.stack([f0_new, f1_new]),
            k.reshape(heads), v.reshape(heads))


def kernel(x_prompt, x_sample, state_lru_h, state_lru_conv, state_ffn_conv, cache_k, cache_v, a_norm, a_w_in, a_conv_w, a_conv_b, a_wr, a_br, a_wi, a_bi, a_lambda, a_w_out, kv_norm, w_kv, k_norm, b_norm, b_wq, q_norm, b_wo, f_norm, f_w_up, f_conv_w, f_conv_b, f_w_down, out_norm):
    w = dict(a_norm=a_norm, a_w_in=a_w_in.astype(BF16), a_conv_w=a_conv_w, a_conv_b=a_conv_b,
             a_gate=_gate_weights(a_wr[0], a_wi[0]), a_br=a_br, a_bi=a_bi, a_lambda=a_lambda,
             a_w_out=a_w_out.astype(BF16), kv_norm=kv_norm, w_kv=w_kv.astype(BF16), k_norm=k_norm,
             b_norm=b_norm, b_wq=b_wq.astype(BF16), q_norm=q_norm, b_wo=b_wo.astype(BF16),
             f_norm=f_norm, f_w_up=f_w_up.astype(BF16), f_conv_w=f_conv_w, f_conv_b=f_conv_b,
             f_w_down=f_w_down.astype(BF16), out_norm=out_norm)
    bp, sp, width = x_prompt.shape
    d_ff = f_w_down.shape[1]
    n_lru = a_w_in.shape[0]
    depth = f_w_up.shape[0]
    zeros = lambda *s: jnp.zeros(s, x_prompt.dtype)
    prompt = _trunk(x_prompt, zeros(n_lru, bp, width), zeros(n_lru, bp, LRU_CONV - 1, width),
                    zeros(depth, bp, FFN_CONV - 1, d_ff), None, w,
                    n_streams=1, t_len=min(512, sp), qkv_rows=min(512, bp * sp))
    bs, ss, _ = x_sample.shape
    sample = _trunk(x_sample, state_lru_h, state_lru_conv, state_ffn_conv, (cache_k, cache_v), w,
                    n_streams=bs, t_len=ss, qkv_rows=bs * ss)
    y_p, h_p, c_p, f_p, k_p, v_p = prompt
    y_s, h_s, c_s, f_s, k_s, v_s = sample
    return (y_p, y_s, h_p, c_p, f_p, k_p, v_p, h_s, c_s, f_s, k_s, v_s)
```

```python
import functools

import jax
import jax.numpy as jnp
from jax import lax
from jax.experimental import pallas as pl
from jax.experimental.pallas import tpu as pltpu

N_HEADS = 8
HEAD_DIM = 128
LRU_C = 8.0
LRU_CONV = 4
FFN_CONV = 3
EPS = 1e-6

SUBLANES = 8
LRU_GROUP = 256
FFN_GROUPS = (256, 768, 1024, 768, 256)
ATT_TILE = 256
ATT_HEADS = 8
ATT_NEW_TILE = 128
ATT_SKEW = 2
VMEM_LIMIT = 56 * 1024 * 1024

BF16 = jnp.bfloat16
F32 = jnp.float32


def _dot(a, b):
    return jnp.dot(a, b, preferred_element_type=F32)


def _rms(x, g):
    return x * lax.rsqrt(jnp.mean(x * x, axis=-1, keepdims=True) + EPS) * g


def _gelu(x):
    return 0.5 * x * (1.0 + jnp.tanh(0.7978845608028654 * (x + 0.044715 * (x * x * x))))


def _sigmoid(x):
    return 0.5 * (jnp.tanh(0.5 * x) + 1.0)


def _log_sigmoid(x):
    return jnp.minimum(x, 0.0) - jnp.log(1.0 + jnp.exp(-jnp.abs(x)))


def _const_spec(shape):
    nd = len(shape)
    return pl.BlockSpec(shape, lambda *_: (0,) * nd, pipeline_mode=pl.Buffered(1))


def _params(sem):
    return pltpu.CompilerParams(dimension_semantics=sem, vmem_limit_bytes=VMEM_LIMIT)


def _carried_conv(cur, tail_ref, w_ref, b_ref, lo, width, n_streams, t_len):
    w_grp = cur.shape[-1]
    outs = []
    for g in range(n_streams):
        x = cur[g * t_len:(g + 1) * t_len]
        xp = jnp.concatenate([tail_ref[g, :, lo:lo + w_grp], x], axis=0)
        y = b_ref[:, lo:lo + w_grp] + w_ref[width - 1:width, lo:lo + w_grp] * x
        for k in range(width - 1):
            shifted = pltpu.roll(xp, width - 1 - k, 0)[SUBLANES:]
            y = y + w_ref[k:k + 1, lo:lo + w_grp] * shifted
        tail_ref[g, :, lo:lo + w_grp] = x[t_len - SUBLANES:]
        outs.append(y)
    return outs[0] if n_streams == 1 else jnp.concatenate(outs, axis=0)


def _rglru_kernel(x_ref, h0_ref, c0_ref, norm_ref, win_ref, cw_ref, cb_ref, wg_ref, br_ref, bi_ref,
                  lam_ref, wout_ref, y_ref, hT_ref, cT_ref, h_s, tail_s, a_s, b_s,
                  *, n_streams, t_len):
    t = pl.program_id(1)
    n_t = pl.num_programs(1)
    width = x_ref.shape[-1]
    rows = n_streams * t_len
    n_grp = t_len // SUBLANES

    @pl.when(t == 0)
    def _():
        h_s[...] = h0_ref[...]
        tail_s[:, SUBLANES - (LRU_CONV - 1):SUBLANES, :] = c0_ref[...]

    x = x_ref[...].reshape(rows, width)
    xn = _rms(x, norm_ref[...]).astype(BF16)
    sub = lax.broadcasted_iota(jnp.int32, (n_grp, SUBLANES, LRU_GROUP), 1)

    def project(cg):
        lo = cg * LRU_GROUP
        return (_dot(xn, win_ref[:, lo:lo + LRU_GROUP]),
                _dot(xn, win_ref[:, width + lo:width + lo + LRU_GROUP]))

    n_cg = width // LRU_GROUP
    acc = jnp.zeros((rows, width), F32)
    nxt = project(0)
    for cg in range(n_cg):
        lo = cg * LRU_GROUP
        gate, rec = nxt
        if cg + 1 < n_cg:
            nxt = project(cg + 1)
        c = _carried_conv(rec, tail_s, cw_ref, cb_ref, lo, LRU_CONV, n_streams, t_len)
        ri = _dot(c.astype(BF16), wg_ref[cg])
        r = _sigmoid(ri[:, :LRU_GROUP] + br_ref[:, lo:lo + LRU_GROUP])
        i = _sigmoid(ri[:, LRU_GROUP:] + bi_ref[:, lo:lo + LRU_GROUP])
        log_a = (LRU_C * r) * _log_sigmoid(lam_ref[:, lo:lo + LRU_GROUP])
        a = jnp.exp(log_a)
        th = jnp.tanh(log_a)
        bt = jnp.sqrt(-2.0 * th / (1.0 - th)) * i * c

        hs = []
        for g in range(n_streams):
            ag = a[g * t_len:(g + 1) * t_len].reshape(n_grp, SUBLANES, LRU_GROUP)
            bg = bt[g * t_len:(g + 1) * t_len].reshape(n_grp, SUBLANES, LRU_GROUP)
            for d in (1, 2, 4):
                keep = sub >= d
                a_sh = pltpu.roll(ag, d, 1)
                b_sh = pltpu.roll(bg, d, 1)
                bg = jnp.where(keep, ag * b_sh + bg, bg)
                ag = jnp.where(keep, ag * a_sh, ag)
            a_s[...] = ag
            b_s[...] = bg

            def group_step(k, h_prev):
                h = a_s[k] * h_prev + b_s[k]
                a_s[k] = h
                return h[SUBLANES - 1:SUBLANES, :]

            h_last = lax.fori_loop(0, n_grp, group_step, h_s[g, :, lo:lo + LRU_GROUP], unroll=True)
            h_s[g, :, lo:lo + LRU_GROUP] = h_last
            hs.append(a_s[...].reshape(t_len, LRU_GROUP))
        h = hs[0] if n_streams == 1 else jnp.concatenate(hs, axis=0)
        acc = acc + _dot((h * _gelu(gate)).astype(BF16), wout_ref[lo:lo + LRU_GROUP, :])

    y_ref[...] = (x + acc).reshape(y_ref.shape)

    @pl.when(t == n_t - 1)
    def _():
        hT_ref[...] = h_s[...]
        cT_ref[...] = tail_s[:, SUBLANES - (LRU_CONV - 1):SUBLANES, :]


def _rglru(x, h0, c0, norm, w_in, conv_w, conv_b, w_gate, br, bi, lam, w_out, *, n_streams, t_len):
    bsz, seq, width = x.shape
    grid = (bsz // n_streams, seq // t_len)
    blk = lambda b, t: (b, t, 0)
    per_b = lambda b, t: (b, 0, 0)
    kern = functools.partial(_rglru_kernel, n_streams=n_streams, t_len=t_len)
    return pl.pallas_call(
        kern,
        grid=grid,
        in_specs=[
            pl.BlockSpec((n_streams, t_len, width), blk),
            pl.BlockSpec((n_streams, 1, width), per_b),
            pl.BlockSpec((n_streams, LRU_CONV - 1, width), per_b),
            _const_spec(norm.shape), _const_spec(w_in.shape), _const_spec(conv_w.shape),
            _const_spec(conv_b.shape), _const_spec(w_gate.shape), _const_spec(br.shape),
            _const_spec(bi.shape), _const_spec(lam.shape), _const_spec(w_out.shape),
        ],
        out_specs=[
            pl.BlockSpec((n_streams, t_len, width), blk),
            pl.BlockSpec((n_streams, 1, width), per_b),
            pl.BlockSpec((n_streams, LRU_CONV - 1, width), per_b),
        ],
        out_shape=[
            jax.ShapeDtypeStruct((bsz, seq, width), F32),
            jax.ShapeDtypeStruct((bsz, 1, width), F32),
            jax.ShapeDtypeStruct((bsz, LRU_CONV - 1, width), F32),
        ],
        scratch_shapes=[
            pltpu.VMEM((n_streams, 1, width), F32),
            pltpu.VMEM((n_streams, SUBLANES, width), F32),
            pltpu.VMEM((t_len // SUBLANES, SUBLANES, LRU_GROUP), F32),
            pltpu.VMEM((t_len // SUBLANES, SUBLANES, LRU_GROUP), F32),
        ],
        compiler_params=_params(("arbitrary", "arbitrary")),
        name="rglru",
    )(x, h0, c0, norm, w_in, conv_w, conv_b, w_gate, br, bi, lam, w_out)


def _ffn_kernel(*refs, n_streams, t_len, has_attn, final_norm):
    refs = list(refs)
    x_ref = refs.pop(0)
    o_ref = refs.pop(0) if has_attn else None
    wo_ref = refs.pop(0) if has_attn else None
    f0_ref, norm_ref, wup_ref, cw_ref, cb_ref, wdown_ref = refs[:6]
    refs = refs[6:]
    onorm_ref = refs.pop(0) if final_norm else None
    y_ref, fT_ref, tail_s = refs

    t = pl.program_id(1)
    n_t = pl.num_programs(1)
    width = x_ref.shape[-1]
    d_ff = wdown_ref.shape[0]
    rows = n_streams * t_len

    @pl.when(t == 0)
    def _():
        tail_s[:, SUBLANES - (FFN_CONV - 1):SUBLANES, :] = f0_ref[...]

    x = x_ref[...].reshape(rows, width)
    if has_attn:
        x = x + _dot(o_ref[...].reshape(rows, width), wo_ref[...])
    xn = _rms(x, norm_ref[...]).astype(BF16)

    bounds = [0]
    for g in FFN_GROUPS:
        bounds.append(bounds[-1] + g)

    def up(fg):
        lo, hi = bounds[fg], bounds[fg + 1]
        return (_dot(xn, wup_ref[:, lo:hi]), _dot(xn, wup_ref[:, d_ff + lo:d_ff + hi]))

    n_grp = len(FFN_GROUPS)
    acc = jnp.zeros((rows, width), F32)
    nxt = up(0)
    for fg in range(n_grp):
        lo, hi = bounds[fg], bounds[fg + 1]
        gpre, u = nxt
        if fg + 1 < n_grp:
            nxt = up(fg + 1)
        gc = _carried_conv(gpre, tail_s, cw_ref, cb_ref, lo, FFN_CONV, n_streams, t_len)
        acc = acc + _dot((_gelu(gc) * u).astype(BF16), wdown_ref[lo:hi, :])

    y = x + acc
    if final_norm:
        y = _rms(y, onorm_ref[...])
    y_ref[...] = y.reshape(y_ref.shape)

    @pl.when(t == n_t - 1)
    def _():
        fT_ref[...] = tail_s[:, SUBLANES - (FFN_CONV - 1):SUBLANES, :]


def _ffn(x, f0, norm, w_up, conv_w, conv_b, w_down, *, n_streams, t_len, attn=None, out_norm=None):
    bsz, seq, width = x.shape
    d_ff = w_down.shape[0]
    assert sum(FFN_GROUPS) == d_ff
    grid = (bsz // n_streams, seq // t_len)
    blk = lambda b, t: (b, t, 0)
    per_b = lambda b, t: (b, 0, 0)
    args = [x]
    in_specs = [pl.BlockSpec((n_streams, t_len, width), blk)]
    if attn is not None:
        o, wo = attn
        args += [o, wo]
        in_specs += [pl.BlockSpec((n_streams, t_len, width), blk), _const_spec(wo.shape)]
    args += [f0, norm, w_up, conv_w, conv_b, w_down]
    in_specs += [pl.BlockSpec((n_streams, FFN_CONV - 1, d_ff), per_b), _const_spec(norm.shape),
                 _const_spec(w_up.shape), _const_spec(conv_w.shape), _const_spec(conv_b.shape),
                 _const_spec(w_down.shape)]
    if out_norm is not None:
        args.append(out_norm)
        in_specs.append(_const_spec(out_norm.shape))
    kern = functools.partial(_ffn_kernel, n_streams=n_streams, t_len=t_len,
                             has_attn=attn is not None, final_norm=out_norm is not None)
    return pl.pallas_call(
        kern,
        grid=grid,
        in_specs=in_specs,
        out_specs=[
            pl.BlockSpec((n_streams, t_len, width), blk),
            pl.BlockSpec((n_streams, FFN_CONV - 1, d_ff), per_b),
        ],
        out_shape=[
            jax.ShapeDtypeStruct((bsz, seq, width), F32),
            jax.ShapeDtypeStruct((bsz, FFN_CONV - 1, d_ff), F32),
        ],
        scratch_shapes=[pltpu.VMEM((n_streams, SUBLANES, d_ff), F32)],
        compiler_params=_params(("arbitrary", "arbitrary")),
        name="ffn_attn" if attn is not None else "ffn",
    )(*args)


def _head_rms(x, g):
    outs = []
    for h in range(N_HEADS):
        outs.append(_rms(x[:, h * HEAD_DIM:(h + 1) * HEAD_DIM], g))
    return jnp.concatenate(outs, axis=-1)


def _qkv_kernel(x_ref, kvn_ref, wkv_ref, kn_ref, bn_ref, wq_ref, qn_ref,
                k_ref, v_ref, kb_ref, vb_ref, qb_ref):
    width = x_ref.shape[-1]
    x = x_ref[...]
    kv = _dot(_rms(x, kvn_ref[...]).astype(BF16), wkv_ref[...])
    k = _head_rms(kv[:, :width], kn_ref[...])
    v = kv[:, width:]
    for h in range(N_HEADS):
        k_ref[:, h, :] = k[:, h * HEAD_DIM:(h + 1) * HEAD_DIM]
        v_ref[:, h, :] = v[:, h * HEAD_DIM:(h + 1) * HEAD_DIM]
    kb_ref[...] = k.astype(BF16)
    vb_ref[...] = v.astype(BF16)
    q = _dot(_rms(x, bn_ref[...]).astype(BF16), wq_ref[...])
    qb_ref[...] = _head_rms(q, qn_ref[...]).astype(BF16)


def _qkv(x2d, kv_norm, w_kv, k_norm, b_norm, wq, q_norm, *, rows):
    n, width = x2d.shape
    blk = pl.BlockSpec((rows, width), lambda i: (i, 0))
    return pl.pallas_call(
        _qkv_kernel,
        grid=(n // rows,),
        in_specs=[blk, _const_spec(kv_norm.shape), _const_spec(w_kv.shape), _const_spec(k_norm.shape),
                  _const_spec(b_norm.shape), _const_spec(wq.shape), _const_spec(q_norm.shape)],
        out_specs=[pl.BlockSpec((rows, N_HEADS, HEAD_DIM), lambda i: (i, 0, 0))] * 2 + [blk] * 3,
        out_shape=[jax.ShapeDtypeStruct((n, N_HEADS, HEAD_DIM), F32)] * 2
        + [jax.ShapeDtypeStruct((n, width), BF16)] * 3,
        compiler_params=_params(("arbitrary",)),
        name="qkv",
    )(x2d, kv_norm, w_kv, k_norm, b_norm, wq, q_norm)


def _later_keys(n):
    j = lax.broadcasted_iota(jnp.int32, (n, n), 0)
    s = lax.broadcasted_iota(jnp.int32, (n, n), 1)
    return (j > s).astype(BF16)


def _sb_tiles(qs, kts, vts, later, carry_ref, acc_ref, mask, first, slots=None):
    n = len(qs)
    slots = list(range(n)) if slots is None else slots
    log_betas, heads_of, stays, ws = {}, {}, {}, {}
    for step in range(n + 2 * ATT_SKEW):
        u = step
        if u < n:
            z = lax.dot_general(qs[u], kts[u], (((1,), (1,)), ((), ())), preferred_element_type=F32)
            y = z * (HEAD_DIM ** -0.5)
            log_beta = jnp.minimum(y, 0.0) - jnp.log(1.0 + jnp.exp(-jnp.abs(y)))
            log_stay = log_beta - y
            if mask is not None:
                log_stay = jnp.where(mask, log_stay, 0.0)
            log_betas[u] = log_beta
            heads_of[u] = log_stay[:, :1]
            stays[u] = log_stay.astype(BF16)
        u = step - ATT_SKEW
        if 0 <= u < n:
            after = _dot(stays.pop(u), later)
            total = after[:, :1] + heads_of.pop(u)
            if first:
                w = jnp.exp(log_betas.pop(u) + after)
                carry_ref[slots[u]] = total
            else:
                carry = carry_ref[slots[u]]
                w = jnp.exp(log_betas.pop(u) + after + carry)
                carry_ref[slots[u]] = carry + total
            if mask is not None:
                w = jnp.where(mask, w, 0.0)
            ws[u] = w.astype(BF16)
        u = step - 2 * ATT_SKEW
        if 0 <= u < n:
            pv = _dot(ws.pop(u), vts[u])
            acc_ref[slots[u]] = pv if first else acc_ref[slots[u]] + pv


def _attn_prompt_kernel(q_ref, k_ref, v_ref, o_ref, carry_s, acc_s):
    i = pl.program_id(2)
    tq = q_ref.shape[1]
    n_heads = q_ref.shape[2] // HEAD_DIM
    later = _later_keys(tq)
    row = lax.broadcasted_iota(jnp.int32, (tq, tq), 0)
    col = lax.broadcasted_iota(jnp.int32, (tq, tq), 1)
    heads = [slice(h * HEAD_DIM, (h + 1) * HEAD_DIM) for h in range(n_heads)]
    qs = [q_ref[0, :, hs] for hs in heads]

    def tiles(js, mask, first):
        kts, vts = [], []
        for j in js:
            start = pl.multiple_of(j * tq, tq)
            kts += [k_ref[0, pl.ds(start, tq), hs] for hs in heads]
            vts += [v_ref[0, pl.ds(start, tq), hs] for hs in heads]
        _sb_tiles(qs * len(js), kts, vts, later, carry_s, acc_s, mask, first,
                  slots=list(range(n_heads)) * len(js))

    tiles([i], col < row, True)

    def pair(n, _):
        tiles([i - 1 - 2 * n, i - 2 - 2 * n], None, False)
        return 0

    lax.fori_loop(0, i // 2, pair, 0)

    @pl.when(i % 2 == 1)
    def _():
        tiles([0], None, False)

    for h, hs in enumerate(heads):
        o_ref[0, :, hs] = acc_s[h].astype(o_ref.dtype)


def _attn_prompt(qb, kb, vb):
    bsz, seq, width = qb.shape
    tq = min(ATT_TILE, seq)
    lanes = ATT_HEADS * HEAD_DIM
    grid = (bsz, width // lanes, seq // tq)
    q_spec = pl.BlockSpec((1, tq, lanes), lambda b, h, i: (b, i, h))
    kv_spec = pl.BlockSpec((1, seq, lanes), lambda b, h, i: (b, 0, h))
    return pl.pallas_call(
        _attn_prompt_kernel,
        grid=grid,
        in_specs=[q_spec, kv_spec, kv_spec],
        out_specs=q_spec,
        out_shape=jax.ShapeDtypeStruct((bsz, seq, width), BF16),
        scratch_shapes=[pltpu.VMEM((ATT_HEADS, tq, 1), F32), pltpu.VMEM((ATT_HEADS, tq, HEAD_DIM), F32)],
        compiler_params=_params(("arbitrary", "arbitrary", "arbitrary")),
        name="attn_prompt",
    )(qb, kb, vb)


def _attn_sample_kernel(qbd_ref, kn_ref, vn_ref, kc_ref, vc_ref, o_ref, carry_s, acc_s, *, t_q):
    j = pl.program_id(1)
    n_j = pl.num_programs(1)
    rows = qbd_ref.shape[1]
    q = qbd_ref[0]

    @pl.when(j == 0)
    def _():
        n_new = kn_ref.shape[1]
        t = lax.broadcasted_iota(jnp.int32, (rows, n_new), 0) % t_q
        s = lax.broadcasted_iota(jnp.int32, (rows, n_new), 1)
        _sb_tiles([q], [kn_ref[0]], [vn_ref[0]], _later_keys(n_new), carry_s, acc_s, s < t, True)

    @pl.when(j > 0)
    def _():
        n_tiles = kc_ref.shape[1] // ATT_TILE
        later = _later_keys(ATT_TILE)

        def tile(n, _):
            start = pl.multiple_of((n_tiles - 1 - n) * ATT_TILE, ATT_TILE)
            kt = kc_ref[0, pl.ds(start, ATT_TILE), :].astype(BF16)
            vt = vc_ref[0, pl.ds(start, ATT_TILE), :].astype(BF16)
            _sb_tiles([q], [kt], [vt], later, carry_s, acc_s, None, False)
            return 0

        lax.fori_loop(0, n_tiles, tile, 0)

    @pl.when(j == n_j - 1)
    def _():
        for h in range(N_HEADS):
            o_ref[0, :, h * HEAD_DIM:(h + 1) * HEAD_DIM] = acc_s[
                0, h * t_q:(h + 1) * t_q, h * HEAD_DIM:(h + 1) * HEAD_DIM].astype(o_ref.dtype)


def _attn_sample(qbd, k_new, v_new, cache_k, cache_v, *, t_q, chunk):
    bsz, rows, width = qbd.shape
    past = cache_k.shape[1]
    n_chunks = past // chunk
    n_new = k_new.shape[1]
    cache_spec = pl.BlockSpec((1, chunk, width), lambda b, j: (b, jnp.minimum(n_chunks - j, n_chunks - 1), 0))
    new_spec = pl.BlockSpec((1, n_new, width), lambda b, j: (b, 0, 0))
    kern = functools.partial(_attn_sample_kernel, t_q=t_q)
    return pl.pallas_call(
        kern,
        grid=(bsz, n_chunks + 1),
        in_specs=[pl.BlockSpec((1, rows, width), lambda b, j: (b, 0, 0)), new_spec, new_spec,
                  cache_spec, cache_spec],
        out_specs=pl.BlockSpec((1, t_q, width), lambda b, j: (b, 0, 0)),
        out_shape=jax.ShapeDtypeStruct((bsz, t_q, width), BF16),
        scratch_shapes=[pltpu.VMEM((1, rows, 1), F32), pltpu.VMEM((1, rows, width), F32)],
        compiler_params=_params(("arbitrary", "arbitrary")),
        name="attn_sample",
    )(qbd, k_new, v_new, cache_k, cache_v)


def _gate_weights(wr, wi):
    per = LRU_GROUP // wr.shape[-1]
    n_grp = wr.shape[0] // per
    eye = jnp.eye(per, dtype=wr.dtype)

    def expand(w):
        w = w.reshape(n_grp, per, w.shape[1], w.shape[2])
        return jnp.einsum('gnde,nm->gndme', w, eye).reshape(n_grp, LRU_GROUP, LRU_GROUP)

    return jnp.concatenate([expand(wr), expand(wi)], axis=-1).astype(BF16)


def _block_diag_queries(qb, t_q):
    bsz = qb.shape[0]
    q = qb.reshape(bsz, t_q, N_HEADS, HEAD_DIM).transpose(0, 2, 1, 3)
    eye = jnp.eye(N_HEADS, dtype=qb.dtype)
    return jnp.einsum('bhtd,hg->bhtgd', q, eye).reshape(bsz, N_HEADS * t_q, N_HEADS * HEAD_DIM)


def _trunk(x, lru_h, lru_conv, ffn_conv, cache, w, *, n_streams, t_len, qkv_rows):
    bsz, seq, width = x.shape
    row = lambda v: v.reshape(1, -1)
    x1, h_new, c_new = _rglru(
        x, lru_h[0][:, None, :], lru_conv[0], row(w['a_norm'][0]), w['a_w_in'][0], w['a_conv_w'][0],
        row(w['a_conv_b'][0]), w['a_gate'], row(w['a_br'][0]), row(w['a_bi'][0]), row(w['a_lambda'][0]),
        w['a_w_out'][0], n_streams=n_streams, t_len=t_len)
    x2, f0_new = _ffn(x1, ffn_conv[0], row(w['f_norm'][0]), w['f_w_up'][0], w['f_conv_w'][0],
                      row(w['f_conv_b'][0]), w['f_w_down'][0], n_streams=n_streams, t_len=t_len)
    k, v, kb, vb, qb = _qkv(x2.reshape(bsz * seq, width), row(w['kv_norm']), w['w_kv'], row(w['k_norm']),
                            row(w['b_norm'][0]), w['b_wq'][0], row(w['q_norm'][0]), rows=qkv_rows)
    shape3 = (bsz, seq, width)
    kb, vb, qb = kb.reshape(shape3), vb.reshape(shape3), qb.reshape(shape3)
    if cache is None:
        o = _attn_prompt(qb, kb, vb)
    else:
        pad = ((0, 0), (0, ATT_NEW_TILE - seq), (0, 0))
        o = _attn_sample(_block_diag_queries(qb, seq), jnp.pad(kb, pad), jnp.pad(vb, pad),
                         cache[0], cache[1], t_q=seq, chunk=1024)
    y, f1_new = _ffn(x2, ffn_conv[1], row(w['f_norm'][1]), w['f_w_up'][1], w['f_conv_w'][1],
                     row(w['f_conv_b'][1]), w['f_w_down'][1], n_streams=n_streams, t_len=t_len,
                     attn=(o, w['b_wo'][0]), out_norm=row(w['out_norm']))
    heads = (bsz, seq, N_HEADS, HEAD_DIM)
    return (y, h_new.reshape(1, bsz, width), c_new[None], jnp.stack([f0_new, f1_new]),
            k.reshape(heads), v.reshape(heads))


def kernel(x_prompt, x_sample, state_lru_h, state_lru_conv, state_ffn_conv, cache_k, cache_v, a_norm, a_w_in, a_conv_w, a_conv_b, a_wr, a_br, a_wi, a_bi, a_lambda, a_w_out, kv_norm, w_kv, k_norm, b_norm, b_wq, q_norm, b_wo, f_norm, f_w_up, f_conv_w, f_conv_b, f_w_down, out_norm):
    w = dict(a_norm=a_norm, a_w_in=a_w_in.astype(BF16), a_conv_w=a_conv_w, a_conv_b=a_conv_b,
             a_gate=_gate_weights(a_wr[0], a_wi[0]), a_br=a_br, a_bi=a_bi, a_lambda=a_lambda,
             a_w_out=a_w_out.astype(BF16), kv_norm=kv_norm, w_kv=w_kv.astype(BF16), k_norm=k_norm,
             b_norm=b_norm, b_wq=b_wq.astype(BF16), q_norm=q_norm, b_wo=b_wo.astype(BF16),
             f_norm=f_norm, f_w_up=f_w_up.astype(BF16), f_conv_w=f_conv_w, f_conv_b=f_conv_b,
             f_w_down=f_w_down.astype(BF16), out_norm=out_norm)
    bp, sp, width = x_prompt.shape
    d_ff = f_w_down.shape[1]
    n_lru = a_w_in.shape[0]
    depth = f_w_up.shape[0]
    zeros = lambda *s: jnp.zeros(s, x_prompt.dtype)
    prompt = _trunk(x_prompt, zeros(n_lru, bp, width), zeros(n_lru, bp, LRU_CONV - 1, width),
                    zeros(depth, bp, FFN_CONV - 1, d_ff), None, w,
                    n_streams=1, t_len=min(512, sp), qkv_rows=min(512, bp * sp))
    bs, ss, _ = x_sample.shape
    sample = _trunk(x_sample, state_lru_h, state_lru_conv, state_ffn_conv,
                    (cache_k.reshape(bs, -1, width), cache_v.reshape(bs, -1, width)), w,
                    n_streams=bs, t_len=ss, qkv_rows=bs * ss)
    y_p, h_p, c_p, f_p, k_p, v_p = prompt
    y_s, h_s, c_s, f_s, k_s, v_s = sample
    return (y_p, y_s, h_p, c_p, f_p, k_p, v_p, h_s, c_s, f_s, k_s, v_s)
```

```python
import functools

import jax
import jax.numpy as jnp
from jax import lax
from jax.experimental import pallas as pl
from jax.experimental.pallas import tpu as pltpu

N_HEADS = 8
HEAD_DIM = 128
LRU_C = 8.0
LRU_CONV = 4
FFN_CONV = 3
EPS = 1e-6

SUBLANES = 8
LRU_GROUP = 256
FFN_GROUPS = (256, 768, 1024, 768, 256)
ATT_TILE = 256
ATT_HEADS = 8
ATT_NEW_TILE = 128
ATT_SKEW = 2
VMEM_LIMIT = 56 * 1024 * 1024

BF16 = jnp.bfloat16
F32 = jnp.float32


def _dot(a, b):
    return jnp.dot(a, b, preferred_element_type=F32)


def _rms(x, g):
    return x * lax.rsqrt(jnp.mean(x * x, axis=-1, keepdims=True) + EPS) * g


def _gelu(x):
    return 0.5 * x * (1.0 + jnp.tanh(0.7978845608028654 * (x + 0.044715 * (x * x * x))))


def _gelu2(x):
    return x * (1.0 + jnp.tanh(x * (0.7978845608028654 + (0.7978845608028654 * 0.044715) * (x * x))))


def _log_sigmoid(x):
    return jnp.minimum(x, 0.0) - jnp.log(1.0 + jnp.exp(-jnp.abs(x)))


def _const_spec(shape):
    nd = len(shape)
    return pl.BlockSpec(shape, lambda *_: (0,) * nd, pipeline_mode=pl.Buffered(1))


def _params(sem):
    return pltpu.CompilerParams(dimension_semantics=sem, vmem_limit_bytes=VMEM_LIMIT)


def _carried_conv(cur, tail_ref, w_ref, b_ref, lo, width, n_streams, t_len):
    w_grp = cur.shape[-1]
    outs = []
    for g in range(n_streams):
        x = cur[g * t_len:(g + 1) * t_len]
        xp = jnp.concatenate([tail_ref[g, :, lo:lo + w_grp], x], axis=0)
        y = b_ref[:, lo:lo + w_grp] + w_ref[width - 1:width, lo:lo + w_grp] * x
        for k in range(width - 1):
            shifted = pltpu.roll(xp, width - 1 - k, 0)[SUBLANES:]
            y = y + w_ref[k:k + 1, lo:lo + w_grp] * shifted
        tail_ref[g, :, lo:lo + w_grp] = x[t_len - SUBLANES:]
        outs.append(y)
    return outs[0] if n_streams == 1 else jnp.concatenate(outs, axis=0)


def _scan_by_groups(a, b, h_prev):
    t_len, width = a.shape
    n_grp = t_len // SUBLANES
    a = a.reshape(n_grp, SUBLANES, width)
    b = b.reshape(n_grp, SUBLANES, width)
    sub = lax.broadcasted_iota(jnp.int32, a.shape, 1)
    for d in (1, 2, 4):
        keep = sub >= d
        a_sh = pltpu.roll(a, d, 1)
        b_sh = pltpu.roll(b, d, 1)
        b = jnp.where(keep, a * b_sh + b, b)
        a = jnp.where(keep, a * a_sh, a)
    hs = []
    for k in range(n_grp):
        h = a[k] * h_prev + b[k]
        h_prev = h[SUBLANES - 1:SUBLANES]
        hs.append(h)
    return jnp.concatenate(hs, axis=0), h_prev


def _rglru_kernel(x_ref, h0_ref, c0_ref, norm_ref, win_ref, cw_ref, cb_ref, wg_ref, br_ref, bi_ref,
                  lam_ref, wout_ref, y_ref, hT_ref, cT_ref, h_s, tail_s,
                  *, n_streams, t_len):
    t = pl.program_id(1)
    n_t = pl.num_programs(1)
    width = x_ref.shape[-1]
    rows = n_streams * t_len

    @pl.when(t == 0)
    def _():
        h_s[...] = h0_ref[...]
        tail_s[:, SUBLANES - (LRU_CONV - 1):SUBLANES, :] = c0_ref[...]

    x = x_ref[...].reshape(rows, width)
    xn = _rms(x, norm_ref[...]).astype(BF16)

    def project(cg):
        lo = cg * LRU_GROUP
        return (_dot(xn, win_ref[:, lo:lo + LRU_GROUP]),
                _dot(xn, win_ref[:, width + lo:width + lo + LRU_GROUP]))

    n_cg = width // LRU_GROUP
    acc = jnp.zeros((rows, width), F32)
    nxt = project(0)
    for cg in range(n_cg):
        lo = cg * LRU_GROUP
        gate, rec = nxt
        if cg + 1 < n_cg:
            nxt = project(cg + 1)
        c = _carried_conv(rec, tail_s, cw_ref, cb_ref, lo, LRU_CONV, n_streams, t_len)
        ri = _dot(c.astype(BF16), wg_ref[cg])
        r2 = jnp.tanh(ri[:, :LRU_GROUP] + br_ref[:, lo:lo + LRU_GROUP]) + 1.0
        i2 = jnp.tanh(ri[:, LRU_GROUP:] + bi_ref[:, lo:lo + LRU_GROUP]) + 1.0
        log_a = r2 * ((0.5 * LRU_C) * _log_sigmoid(lam_ref[:, lo:lo + LRU_GROUP]))
        a = jnp.exp(log_a)
        th = jnp.tanh(log_a)
        u = (-0.5 * th) * (1.0 / (1.0 - th))
        bt = jnp.where(u > 0.0, u * lax.rsqrt(u), 0.0) * i2 * c

        hs = []
        for g in range(n_streams):
            ag = a[g * t_len:(g + 1) * t_len]
            bg = bt[g * t_len:(g + 1) * t_len]
            h_prev = h_s[g, :, lo:lo + LRU_GROUP]
            hg, h_last = _scan_by_groups(ag, bg, h_prev)
            h_s[g, :, lo:lo + LRU_GROUP] = h_last
            hs.append(hg)
        h = hs[0] if n_streams == 1 else jnp.concatenate(hs, axis=0)
        acc = acc + _dot((h * _gelu2(gate)).astype(BF16), wout_ref[lo:lo + LRU_GROUP, :])

    y_ref[...] = (x + acc).reshape(y_ref.shape)

    @pl.when(t == n_t - 1)
    def _():
        hT_ref[...] = h_s[...]
        cT_ref[...] = tail_s[:, SUBLANES - (LRU_CONV - 1):SUBLANES, :]


def _rglru(x, h0, c0, norm, w_in, conv_w, conv_b, w_gate, br, bi, lam, w_out, *, n_streams, t_len):
    bsz, seq, width = x.shape
    grid = (bsz // n_streams, seq // t_len)
    blk = lambda b, t: (b, t, 0)
    per_b = lambda b, t: (b, 0, 0)
    kern = functools.partial(_rglru_kernel, n_streams=n_streams, t_len=t_len)
    return pl.pallas_call(
        kern,
        grid=grid,
        in_specs=[
            pl.BlockSpec((n_streams, t_len, width), blk),
            pl.BlockSpec((n_streams, 1, width), per_b),
            pl.BlockSpec((n_streams, LRU_CONV - 1, width), per_b),
            _const_spec(norm.shape), _const_spec(w_in.shape), _const_spec(conv_w.shape),
            _const_spec(conv_b.shape), _const_spec(w_gate.shape), _const_spec(br.shape),
            _const_spec(bi.shape), _const_spec(lam.shape), _const_spec(w_out.shape),
        ],
        out_specs=[
            pl.BlockSpec((n_streams, t_len, width), blk),
            pl.BlockSpec((n_streams, 1, width), per_b),
            pl.BlockSpec((n_streams, LRU_CONV - 1, width), per_b),
        ],
        out_shape=[
            jax.ShapeDtypeStruct((bsz, seq, width), F32),
            jax.ShapeDtypeStruct((bsz, 1, width), F32),
            jax.ShapeDtypeStruct((bsz, LRU_CONV - 1, width), F32),
        ],
        scratch_shapes=[
            pltpu.VMEM((n_streams, 1, width), F32),
            pltpu.VMEM((n_streams, SUBLANES, width), F32),
        ],
        compiler_params=_params(("arbitrary", "arbitrary")),
        name="rglru",
    )(x, h0, c0, norm, w_in, conv_w, conv_b, w_gate, br, bi, lam, w_out)


def _ffn_kernel(*refs, n_streams, t_len, has_attn, final_norm):
    refs = list(refs)
    x_ref = refs.pop(0)
    o_ref = refs.pop(0) if has_attn else None
    wo_ref = refs.pop(0) if has_attn else None
    f0_ref, norm_ref, wup_ref, cw_ref, cb_ref, wdown_ref = refs[:6]
    refs = refs[6:]
    onorm_ref = refs.pop(0) if final_norm else None
    y_ref, fT_ref, tail_s = refs

    t = pl.program_id(1)
    n_t = pl.num_programs(1)
    width = x_ref.shape[-1]
    d_ff = wdown_ref.shape[0]
    rows = n_streams * t_len

    @pl.when(t == 0)
    def _():
        tail_s[:, SUBLANES - (FFN_CONV - 1):SUBLANES, :] = f0_ref[...]

    x = x_ref[...].reshape(rows, width)
    if has_attn:
        x = x + _dot(o_ref[...].reshape(rows, width), wo_ref[...])
    xn = _rms(x, norm_ref[...]).astype(BF16)

    bounds = [0]
    for g in FFN_GROUPS:
        bounds.append(bounds[-1] + g)

    def up(fg):
        lo, hi = bounds[fg], bounds[fg + 1]
        return (_dot(xn, wup_ref[:, lo:hi]), _dot(xn, wup_ref[:, d_ff + lo:d_ff + hi]))

    n_grp = len(FFN_GROUPS)
    acc = jnp.zeros((rows, width), F32)
    nxt = up(0)
    for fg in range(n_grp):
        lo, hi = bounds[fg], bounds[fg + 1]
        gpre, u = nxt
        if fg + 1 < n_grp:
            nxt = up(fg + 1)
        gc = _carried_conv(gpre, tail_s, cw_ref, cb_ref, lo, FFN_CONV, n_streams, t_len)
        acc = acc + _dot((_gelu(gc) * u).astype(BF16), wdown_ref[lo:hi, :])

    y = x + acc
    if final_norm:
        y = _rms(y, onorm_ref[...])
    y_ref[...] = y.reshape(y_ref.shape)

    @pl.when(t == n_t - 1)
    def _():
        fT_ref[...] = tail_s[:, SUBLANES - (FFN_CONV - 1):SUBLANES, :]


def _ffn(x, f0, norm, w_up, conv_w, conv_b, w_down, *, n_streams, t_len, attn=None, out_norm=None):
    bsz, seq, width = x.shape
    d_ff = w_down.shape[0]
    assert sum(FFN_GROUPS) == d_ff
    grid = (bsz // n_streams, seq // t_len)
    blk = lambda b, t: (b, t, 0)
    per_b = lambda b, t: (b, 0, 0)
    args = [x]
    in_specs = [pl.BlockSpec((n_streams, t_len, width), blk)]
    if attn is not None:
        o, wo = attn
        args += [o, wo]
        in_specs += [pl.BlockSpec((n_streams, t_len, width), blk), _const_spec(wo.shape)]
    args += [f0, norm, w_up, conv_w, conv_b, w_down]
    in_specs += [pl.BlockSpec((n_streams, FFN_CONV - 1, d_ff), per_b), _const_spec(norm.shape),
                 _const_spec(w_up.shape), _const_spec(conv_w.shape), _const_spec(conv_b.shape),
                 _const_spec(w_down.shape)]
    if out_norm is not None:
        args.append(out_norm)
        in_specs.append(_const_spec(out_norm.shape))
    kern = functools.partial(_ffn_kernel, n_streams=n_streams, t_len=t_len,
                             has_attn=attn is not None, final_norm=out_norm is not None)
    return pl.pallas_call(
        kern,
        grid=grid,
        in_specs=in_specs,
        out_specs=[
            pl.BlockSpec((n_streams, t_len, width), blk),
            pl.BlockSpec((n_streams, FFN_CONV - 1, d_ff), per_b),
        ],
        out_shape=[
            jax.ShapeDtypeStruct((bsz, seq, width), F32),
            jax.ShapeDtypeStruct((bsz, FFN_CONV - 1, d_ff), F32),
        ],
        scratch_shapes=[pltpu.VMEM((n_streams, SUBLANES, d_ff), F32)],
        compiler_params=_params(("arbitrary", "arbitrary")),
        name="ffn_attn" if attn is not None else "ffn",
    )(*args)


def _head_rms(x, g):
    outs = []
    for h in range(N_HEADS):
        outs.append(_rms(x[:, h * HEAD_DIM:(h + 1) * HEAD_DIM], g))
    return jnp.concatenate(outs, axis=-1)


def _split_heads(x):
    per_head = jnp.stack([x[:, h * HEAD_DIM:(h + 1) * HEAD_DIM] for h in range(N_HEADS)], axis=0)
    return jnp.transpose(per_head, (1, 0, 2))


def _merge_heads(x):
    per_head = jnp.transpose(x, (1, 0, 2))
    return jnp.concatenate([per_head[h] for h in range(N_HEADS)], axis=1)


def _qkv_kernel(x_ref, kvn_ref, wkv_ref, kn_ref, bn_ref, wq_ref, qn_ref,
                k_ref, v_ref, kb_ref, vb_ref, qb_ref):
    width = x_ref.shape[-1]
    x = x_ref[...]
    kv = _dot(_rms(x, kvn_ref[...]).astype(BF16), wkv_ref[...])
    k = _head_rms(kv[:, :width], kn_ref[...])
    v = kv[:, width:]
    k_ref[...] = _split_heads(k)
    v_ref[...] = _split_heads(v)
    kb_ref[...] = k.astype(BF16)
    vb_ref[...] = v.astype(BF16)
    q = _dot(_rms(x, bn_ref[...]).astype(BF16), wq_ref[...])
    qb_ref[...] = (_head_rms(q, qn_ref[...]) * (HEAD_DIM ** -0.5)).astype(BF16)


def _qkv(x2d, kv_norm, w_kv, k_norm, b_norm, wq, q_norm, *, rows):
    n, width = x2d.shape
    blk = pl.BlockSpec((rows, width), lambda i: (i, 0))
    return pl.pallas_call(
        _qkv_kernel,
        grid=(n // rows,),
        in_specs=[blk, _const_spec(kv_norm.shape), _const_spec(w_kv.shape), _const_spec(k_norm.shape),
                  _const_spec(b_norm.shape), _const_spec(wq.shape), _const_spec(q_norm.shape)],
        out_specs=[pl.BlockSpec((rows, N_HEADS, HEAD_DIM), lambda i: (i, 0, 0))] * 2 + [blk] * 3,
        out_shape=[jax.ShapeDtypeStruct((n, N_HEADS, HEAD_DIM), F32)] * 2
        + [jax.ShapeDtypeStruct((n, width), BF16)] * 3,
        compiler_params=_params(("arbitrary",)),
        name="qkv",
    )(x2d, kv_norm, w_kv, k_norm, b_norm, wq, q_norm)


def _later_keys(n):
    j = lax.broadcasted_iota(jnp.int32, (n, n), 0)
    s = lax.broadcasted_iota(jnp.int32, (n, n), 1)
    return (j > s).astype(BF16)


def _sb_tiles(qs, kts, vts, later, carry_ref, acc_ref, mask, first, slots=None):
    n = len(qs)
    slots = list(range(n)) if slots is None else slots
    log_betas, heads_of, stays, ws = {}, {}, {}, {}
    for step in range(n + 2 * ATT_SKEW):
        u = step
        if u < n:
            y = lax.dot_general(qs[u], kts[u], (((1,), (1,)), ((), ())), preferred_element_type=F32)
            log_beta = jnp.minimum(y, 0.0) - jnp.log(1.0 + jnp.exp(-jnp.abs(y)))
            log_stay = log_beta - y
            if mask is not None:
                log_stay = jnp.where(mask, log_stay, 0.0)
            log_betas[u] = log_beta
            heads_of[u] = log_stay[:, :1]
            stays[u] = log_stay.astype(BF16)
        u = step - ATT_SKEW
        if 0 <= u < n:
            after = _dot(stays.pop(u), later)
            total = after[:, :1] + heads_of.pop(u)
            if first:
                w = jnp.exp(log_betas.pop(u) + after)
                carry_ref[slots[u]] = total
            else:
                carry = carry_ref[slots[u]]
                w = jnp.exp(log_betas.pop(u) + after + carry)
                carry_ref[slots[u]] = carry + total
            if mask is not None:
                w = jnp.where(mask, w, 0.0)
            ws[u] = w.astype(BF16)
        u = step - 2 * ATT_SKEW
        if 0 <= u < n:
            pv = _dot(ws.pop(u), vts[u])
            acc_ref[slots[u]] = pv if first else acc_ref[slots[u]] + pv


def _attn_prompt_kernel(q_ref, k_ref, v_ref, o_ref, carry_s, acc_s):
    i = pl.program_id(2)
    tq = q_ref.shape[1]
    n_heads = q_ref.shape[2] // HEAD_DIM
    later = _later_keys(tq)
    row = lax.broadcasted_iota(jnp.int32, (tq, tq), 0)
    col = lax.broadcasted_iota(jnp.int32, (tq, tq), 1)
    heads = [slice(h * HEAD_DIM, (h + 1) * HEAD_DIM) for h in range(n_heads)]
    qs = [q_ref[0, :, hs] for hs in heads]

    def tiles(js, mask, first):
        kts, vts = [], []
        for j in js:
            start = pl.multiple_of(j * tq, tq)
            kts += [k_ref[0, pl.ds(start, tq), hs] for hs in heads]
            vts += [v_ref[0, pl.ds(start, tq), hs] for hs in heads]
        _sb_tiles(qs * len(js), kts, vts, later, carry_s, acc_s, mask, first,
                  slots=list(range(n_heads)) * len(js))

    tiles([i], col < row, True)

    def pair(n, _):
        tiles([i - 1 - 2 * n, i - 2 - 2 * n], None, False)
        return 0

    lax.fori_loop(0, i // 2, pair, 0)

    @pl.when(i % 2 == 1)
    def _():
        tiles([0], None, False)

    for h, hs in enumerate(heads):
        o_ref[0, :, hs] = acc_s[h].astype(o_ref.dtype)


def _attn_prompt(qb, kb, vb):
    bsz, seq, width = qb.shape
    tq = min(ATT_TILE, seq)
    lanes = ATT_HEADS * HEAD_DIM
    grid = (bsz, width // lanes, seq // tq)
    q_spec = pl.BlockSpec((1, tq, lanes), lambda b, h, i: (b, i, h))
    kv_spec = pl.BlockSpec((1, seq, lanes), lambda b, h, i: (b, 0, h))
    return pl.pallas_call(
        _attn_prompt_kernel,
        grid=grid,
        in_specs=[q_spec, kv_spec, kv_spec],
        out_specs=q_spec,
        out_shape=jax.ShapeDtypeStruct((bsz, seq, width), BF16),
        scratch_shapes=[pltpu.VMEM((ATT_HEADS, tq, 1), F32), pltpu.VMEM((ATT_HEADS, tq, HEAD_DIM), F32)],
        compiler_params=_params(("arbitrary", "arbitrary", "arbitrary")),
        name="attn_prompt",
    )(qb, kb, vb)


def _attn_sample_kernel(qbd_ref, kn_ref, vn_ref, kc_ref, vc_ref, o_ref, carry_s, acc_s, *, t_q):
    j = pl.program_id(1)
    n_j = pl.num_programs(1)
    rows = qbd_ref.shape[1]
    q = qbd_ref[0]

    @pl.when(j == 0)
    def _():
        n_new = kn_ref.shape[1]
        t = lax.broadcasted_iota(jnp.int32, (rows, n_new), 0) % t_q
        s = lax.broadcasted_iota(jnp.int32, (rows, n_new), 1)
        _sb_tiles([q], [kn_ref[0]], [vn_ref[0]], _later_keys(n_new), carry_s, acc_s, s < t, True)

    @pl.when(j > 0)
    def _():
        n_tiles = kc_ref.shape[1] // ATT_TILE
        later = _later_keys(ATT_TILE)

        def tile(n, _):
            start = pl.multiple_of((n_tiles - 1 - n) * ATT_TILE, ATT_TILE)
            kt = _merge_heads(kc_ref[0, pl.ds(start, ATT_TILE)]).astype(BF16)
            vt = _merge_heads(vc_ref[0, pl.ds(start, ATT_TILE)]).astype(BF16)
            _sb_tiles([q], [kt], [vt], later, carry_s, acc_s, None, False)
            return 0

        lax.fori_loop(0, n_tiles, tile, 0)

    @pl.when(j == n_j - 1)
    def _():
        for h in range(N_HEADS):
            o_ref[0, :, h * HEAD_DIM:(h + 1) * HEAD_DIM] = acc_s[
                0, h * t_q:(h + 1) * t_q, h * HEAD_DIM:(h + 1) * HEAD_DIM].astype(o_ref.dtype)


def _attn_sample(qbd, k_new, v_new, cache_k, cache_v, *, t_q, chunk):
    bsz, rows, width = qbd.shape
    past = cache_k.shape[1]
    n_chunks = past // chunk
    n_new = k_new.shape[1]
    cache_spec = pl.BlockSpec((1, chunk, N_HEADS, HEAD_DIM),
                              lambda b, j: (b, jnp.minimum(n_chunks - j, n_chunks - 1), 0, 0))
    new_spec = pl.BlockSpec((1, n_new, width), lambda b, j: (b, 0, 0))
    kern = functools.partial(_attn_sample_kernel, t_q=t_q)
    return pl.pallas_call(
        kern,
        grid=(bsz, n_chunks + 1),
        in_specs=[pl.BlockSpec((1, rows, width), lambda b, j: (b, 0, 0)), new_spec, new_spec,
                  cache_spec, cache_spec],
        out_specs=pl.BlockSpec((1, t_q, width), lambda b, j: (b, 0, 0)),
        out_shape=jax.ShapeDtypeStruct((bsz, t_q, width), BF16),
        scratch_shapes=[pltpu.VMEM((1, rows, 1), F32), pltpu.VMEM((1, rows, width), F32)],
        compiler_params=_params(("arbitrary", "arbitrary")),
        name="attn_sample",
    )(qbd, k_new, v_new, cache_k, cache_v)


def _gate_weights(wr, wi):
    per = LRU_GROUP // wr.shape[-1]
    n_grp = wr.shape[0] // per
    eye = jnp.eye(per, dtype=wr.dtype)

    def expand(w):
        w = w.reshape(n_grp, per, w.shape[1], w.shape[2])
        return jnp.einsum('gnde,nm->gndme', w, eye).reshape(n_grp, LRU_GROUP, LRU_GROUP)

    return (0.5 * jnp.concatenate([expand(wr), expand(wi)], axis=-1)).astype(BF16)


def _block_diag_queries(qb, t_q):
    bsz = qb.shape[0]
    q = qb.reshape(bsz, t_q, N_HEADS, HEAD_DIM).transpose(0, 2, 1, 3)
    eye = jnp.eye(N_HEADS, dtype=qb.dtype)
    return jnp.einsum('bhtd,hg->bhtgd', q, eye).reshape(bsz, N_HEADS * t_q, N_HEADS * HEAD_DIM)


def _trunk(x, lru_h, lru_conv, ffn_conv, cache, w, *, n_streams, t_len, qkv_rows):
    bsz, seq, width = x.shape
    row = lambda v: v.reshape(1, -1)
    x1, h_new, c_new = _rglru(
        x, lru_h[0][:, None, :], lru_conv[0], row(w['a_norm'][0]), w['a_w_in'][0], w['a_conv_w'][0],
        row(w['a_conv_b'][0]), w['a_gate'], row(0.5 * w['a_br'][0]), row(0.5 * w['a_bi'][0]),
        row(w['a_lambda'][0]), w['a_w_out_half'][0], n_streams=n_streams, t_len=t_len)
    x2, f0_new = _ffn(x1, ffn_conv[0], row(w['f_norm'][0]), w['f_w_up'][0], w['f_conv_w'][0],
                      row(w['f_conv_b'][0]), w['f_w_down'][0], n_streams=n_streams, t_len=t_len)
    k, v, kb, vb, qb = _qkv(x2.reshape(bsz * seq, width), row(w['kv_norm']), w['w_kv'], row(w['k_norm']),
                            row(w['b_norm'][0]), w['b_wq'][0], row(w['q_norm'][0]), rows=qkv_rows)
    shape3 = (bsz, seq, width)
    kb, vb, qb = kb.reshape(shape3), vb.reshape(shape3), qb.reshape(shape3)
    if cache is None:
        o = _attn_prompt(qb, kb, vb)
    else:
        pad = ((0, 0), (0, ATT_NEW_TILE - seq), (0, 0))
        o = _attn_sample(_block_diag_queries(qb, seq), jnp.pad(kb, pad), jnp.pad(vb, pad),
                         cache[0], cache[1], t_q=seq, chunk=1024)
    y, f1_new = _ffn(x2, ffn_conv[1], row(w['f_norm'][1]), w['f_w_up'][1], w['f_conv_w'][1],
                     row(w['f_conv_b'][1]), w['f_w_down'][1], n_streams=n_streams, t_len=t_len,
                     attn=(o, w['b_wo'][0]), out_norm=row(w['out_norm']))
    heads = (bsz, seq, N_HEADS, HEAD_DIM)
    return (y, h_new.reshape(1, bsz, width), c_new[None], jnp.stack([f0_new, f1_new]),
            k.reshape(heads), v.reshape(heads))


def kernel(x_prompt, x_sample, state_lru_h, state_lru_conv, state_ffn_conv, cache_k, cache_v, a_norm, a_w_in, a_conv_w, a_conv_b, a_wr, a_br, a_wi, a_bi, a_lambda, a_w_out, kv_norm, w_kv, k_norm, b_norm, b_wq, q_norm, b_wo, f_norm, f_w_up, f_conv_w, f_conv_b, f_w_down, out_norm):
    w = dict(a_norm=a_norm, a_w_in=a_w_in.astype(BF16), a_conv_w=a_conv_w, a_conv_b=a_conv_b,
             a_gate=_gate_weights(a_wr[0], a_wi[0]), a_br=a_br, a_bi=a_bi, a_lambda=a_lambda,
             a_w_out_half=(0.5 * a_w_out).astype(BF16), kv_norm=kv_norm, w_kv=w_kv.astype(BF16), k_norm=k_norm,
             b_norm=b_norm, b_wq=b_wq.astype(BF16), q_norm=q_norm, b_wo=b_wo.astype(BF16),
             f_norm=f_norm, f_w_up=f_w_up.astype(BF16), f_conv_w=f_conv_w, f_conv_b=f_conv_b,
             f_w_down=f_w_down.astype(BF16), out_norm=out_norm)
    bp, sp, width = x_prompt.shape
    d_ff = f_w_down.shape[1]
    n_lru = a_w_in.shape[0]
    depth = f_w_up.shape[0]
    zeros = lambda *s: jnp.zeros(s, x_prompt.dtype)
    prompt = _trunk(x_prompt, zeros(n_lru, bp, width), zeros(n_lru, bp, LRU_CONV - 1, width),
                    zeros(depth, bp, FFN_CONV - 1, d_ff), None, w,
                    n_streams=1, t_len=min(512, sp), qkv_rows=min(512, bp * sp))
    bs, ss, _ = x_sample.shape
    sample = _trunk(x_sample, state_lru_h, state_lru_conv, state_ffn_conv, (cache_k, cache_v), w,
                    n_streams=bs, t_len=ss, qkv_rows=bs * ss)
    y_p, h_p, c_p, f_p, k_p, v_p = prompt
    y_s, h_s, c_s, f_s, k_s, v_s = sample
    return (y_p, y_s, h_p, c_p, f_p, k_p, v_p, h_s, c_s, f_s, k_s, v_s)
```

```python
import functools
import math

import jax
import jax.numpy as jnp
from jax import lax
from jax.experimental import pallas as pl
from jax.experimental.pallas import tpu as pltpu

N_HEADS = 8
HEAD_DIM = 128
LRU_C = 8.0
LRU_CONV = 4
FFN_CONV = 3
EPS = 1e-6

SUBLANES = 8
LRU_GROUP = 256
FFN_GROUPS = (256, 768, 1024, 768, 256)
ATT_TILE = 256
ATT_HEADS = 8
ATT_NEW_TILE = 128
ATT_SKEW = 2
VMEM_LIMIT = 56 * 1024 * 1024

LOG2_E = math.log2(math.e)
SCORE_SCALE = HEAD_DIM ** -0.5 * LOG2_E

BF16 = jnp.bfloat16
F32 = jnp.float32


def _dot(a, b):
    return jnp.dot(a, b, preferred_element_type=F32)


def _rms(x, g):
    return x * lax.rsqrt(jnp.mean(x * x, axis=-1, keepdims=True) + EPS) * g


def _gelu(x):
    return 0.5 * x * (1.0 + jnp.tanh(0.7978845608028654 * (x + 0.044715 * (x * x * x))))


def _gelu2(x):
    return x * (1.0 + jnp.tanh(x * (0.7978845608028654 + (0.7978845608028654 * 0.044715) * (x * x))))


def _log_sigmoid(x):
    return jnp.minimum(x, 0.0) - jnp.log(1.0 + jnp.exp(-jnp.abs(x)))


def _const_spec(shape):
    nd = len(shape)
    return pl.BlockSpec(shape, lambda *_: (0,) * nd, pipeline_mode=pl.Buffered(1))


def _params(sem):
    return pltpu.CompilerParams(dimension_semantics=sem, vmem_limit_bytes=VMEM_LIMIT)


def _carried_conv(cur, tail_ref, w_ref, b_ref, lo, width, n_streams, t_len):
    w_grp = cur.shape[-1]
    outs = []
    for g in range(n_streams):
        x = cur[g * t_len:(g + 1) * t_len]
        xp = jnp.concatenate([tail_ref[g, :, lo:lo + w_grp], x], axis=0)
        y = b_ref[:, lo:lo + w_grp] + w_ref[width - 1:width, lo:lo + w_grp] * x
        for k in range(width - 1):
            shifted = pltpu.roll(xp, width - 1 - k, 0)[SUBLANES:]
            y = y + w_ref[k:k + 1, lo:lo + w_grp] * shifted
        tail_ref[g, :, lo:lo + w_grp] = x[t_len - SUBLANES:]
        outs.append(y)
    return outs[0] if n_streams == 1 else jnp.concatenate(outs, axis=0)


def _scan_by_groups(a, b, h_prev):
    t_len, width = a.shape
    n_grp = t_len // SUBLANES
    a = a.reshape(n_grp, SUBLANES, width)
    b = b.reshape(n_grp, SUBLANES, width)
    sub = lax.broadcasted_iota(jnp.int32, a.shape, 1)
    for d in (1, 2, 4):
        keep = sub >= d
        a_sh = pltpu.roll(a, d, 1)
        b_sh = pltpu.roll(b, d, 1)
        b = jnp.where(keep, a * b_sh + b, b)
        a = jnp.where(keep, a * a_sh, a)
    hs = []
    for k in range(n_grp):
        h = a[k] * h_prev + b[k]
        h_prev = h[SUBLANES - 1:SUBLANES]
        hs.append(h)
    return jnp.concatenate(hs, axis=0), h_prev


def _rglru_kernel(x_ref, h0_ref, c0_ref, norm_ref, win_ref, cw_ref, cb_ref, wg_ref, br_ref, bi_ref,
                  lam_ref, wout_ref, y_ref, hT_ref, cT_ref, h_s, tail_s,
                  *, n_streams, t_len):
    t = pl.program_id(1)
    n_t = pl.num_programs(1)
    width = x_ref.shape[-1]
    rows = n_streams * t_len

    @pl.when(t == 0)
    def _():
        h_s[...] = h0_ref[...]
        tail_s[:, SUBLANES - (LRU_CONV - 1):SUBLANES, :] = c0_ref[...]

    x = x_ref[...].reshape(rows, width)
    xn = _rms(x, norm_ref[...]).astype(BF16)

    def project(cg):
        lo = cg * LRU_GROUP
        return (_dot(xn, win_ref[:, lo:lo + LRU_GROUP]),
                _dot(xn, win_ref[:, width + lo:width + lo + LRU_GROUP]))

    n_cg = width // LRU_GROUP
    acc = jnp.zeros((rows, width), F32)
    nxt = project(0)
    for cg in range(n_cg):
        lo = cg * LRU_GROUP
        gate, rec = nxt
        if cg + 1 < n_cg:
            nxt = project(cg + 1)
        c = _carried_conv(rec, tail_s, cw_ref, cb_ref, lo, LRU_CONV, n_streams, t_len)
        ri = _dot(c.astype(BF16), wg_ref[cg])
        r2 = jnp.tanh(ri[:, :LRU_GROUP] + br_ref[:, lo:lo + LRU_GROUP]) + 1.0
        i2 = jnp.tanh(ri[:, LRU_GROUP:] + bi_ref[:, lo:lo + LRU_GROUP]) + 1.0
        log_a = r2 * ((0.5 * LRU_C) * _log_sigmoid(lam_ref[:, lo:lo + LRU_GROUP]))
        a = jnp.exp(log_a)
        th = jnp.tanh(log_a)
        u = (-0.5 * th) * (1.0 / (1.0 - th))
        bt = jnp.where(u > 0.0, u * lax.rsqrt(u), 0.0) * i2 * c

        hs = []
        for g in range(n_streams):
            ag = a[g * t_len:(g + 1) * t_len]
            bg = bt[g * t_len:(g + 1) * t_len]
            h_prev = h_s[g, :, lo:lo + LRU_GROUP]
            hg, h_last = _scan_by_groups(ag, bg, h_prev)
            h_s[g, :, lo:lo + LRU_GROUP] = h_last
            hs.append(hg)
        h = hs[0] if n_streams == 1 else jnp.concatenate(hs, axis=0)
        acc = acc + _dot((h * _gelu2(gate)).astype(BF16), wout_ref[lo:lo + LRU_GROUP, :])

    y_ref[...] = (x + acc).reshape(y_ref.shape)

    @pl.when(t == n_t - 1)
    def _():
        hT_ref[...] = h_s[...]
        cT_ref[...] = tail_s[:, SUBLANES - (LRU_CONV - 1):SUBLANES, :]


def _rglru(x, h0, c0, norm, w_in, conv_w, conv_b, w_gate, br, bi, lam, w_out, *, n_streams, t_len):
    bsz, seq, width = x.shape
    grid = (bsz // n_streams, seq // t_len)
    blk = lambda b, t: (b, t, 0)
    per_b = lambda b, t: (b, 0, 0)
    kern = functools.partial(_rglru_kernel, n_streams=n_streams, t_len=t_len)
    return pl.pallas_call(
        kern,
        grid=grid,
        in_specs=[
            pl.BlockSpec((n_streams, t_len, width), blk),
            pl.BlockSpec((n_streams, 1, width), per_b),
            pl.BlockSpec((n_streams, LRU_CONV - 1, width), per_b),
            _const_spec(norm.shape), _const_spec(w_in.shape), _const_spec(conv_w.shape),
            _const_spec(conv_b.shape), _const_spec(w_gate.shape), _const_spec(br.shape),
            _const_spec(bi.shape), _const_spec(lam.shape), _const_spec(w_out.shape),
        ],
        out_specs=[
            pl.BlockSpec((n_streams, t_len, width), blk),
            pl.BlockSpec((n_streams, 1, width), per_b),
            pl.BlockSpec((n_streams, LRU_CONV - 1, width), per_b),
        ],
        out_shape=[
            jax.ShapeDtypeStruct((bsz, seq, width), F32),
            jax.ShapeDtypeStruct((bsz, 1, width), F32),
            jax.ShapeDtypeStruct((bsz, LRU_CONV - 1, width), F32),
        ],
        scratch_shapes=[
            pltpu.VMEM((n_streams, 1, width), F32),
            pltpu.VMEM((n_streams, SUBLANES, width), F32),
        ],
        compiler_params=_params(("arbitrary", "arbitrary")),
        name="rglru",
    )(x, h0, c0, norm, w_in, conv_w, conv_b, w_gate, br, bi, lam, w_out)


def _ffn_kernel(*refs, n_streams, t_len, has_attn, final_norm):
    refs = list(refs)
    x_ref = refs.pop(0)
    o_ref = refs.pop(0) if has_attn else None
    wo_ref = refs.pop(0) if has_attn else None
    f0_ref, norm_ref, wup_ref, cw_ref, cb_ref, wdown_ref = refs[:6]
    refs = refs[6:]
    onorm_ref = refs.pop(0) if final_norm else None
    y_ref, fT_ref, tail_s = refs

    t = pl.program_id(1)
    n_t = pl.num_programs(1)
    width = x_ref.shape[-1]
    d_ff = wdown_ref.shape[0]
    rows = n_streams * t_len

    @pl.when(t == 0)
    def _():
        tail_s[:, SUBLANES - (FFN_CONV - 1):SUBLANES, :] = f0_ref[...]

    x = x_ref[...].reshape(rows, width)
    if has_attn:
        x = x + _dot(o_ref[...].reshape(rows, width), wo_ref[...])
    xn = _rms(x, norm_ref[...]).astype(BF16)

    bounds = [0]
    for g in FFN_GROUPS:
        bounds.append(bounds[-1] + g)

    def up(fg):
        lo, hi = bounds[fg], bounds[fg + 1]
        return (_dot(xn, wup_ref[:, lo:hi]), _dot(xn, wup_ref[:, d_ff + lo:d_ff + hi]))

    n_grp = len(FFN_GROUPS)
    acc = jnp.zeros((rows, width), F32)
    nxt = up(0)
    for fg in range(n_grp):
        lo, hi = bounds[fg], bounds[fg + 1]
        gpre, u = nxt
        if fg + 1 < n_grp:
            nxt = up(fg + 1)
        gc = _carried_conv(gpre, tail_s, cw_ref, cb_ref, lo, FFN_CONV, n_streams, t_len)
        acc = acc + _dot((_gelu(gc) * u).astype(BF16), wdown_ref[lo:hi, :])

    y = x + acc
    if final_norm:
        y = _rms(y, onorm_ref[...])
    y_ref[...] = y.reshape(y_ref.shape)

    @pl.when(t == n_t - 1)
    def _():
        fT_ref[...] = tail_s[:, SUBLANES - (FFN_CONV - 1):SUBLANES, :]


def _ffn(x, f0, norm, w_up, conv_w, conv_b, w_down, *, n_streams, t_len, attn=None, out_norm=None):
    bsz, seq, width = x.shape
    d_ff = w_down.shape[0]
    assert sum(FFN_GROUPS) == d_ff
    grid = (bsz // n_streams, seq // t_len)
    blk = lambda b, t: (b, t, 0)
    per_b = lambda b, t: (b, 0, 0)
    args = [x]
    in_specs = [pl.BlockSpec((n_streams, t_len, width), blk)]
    if attn is not None:
        o, wo = attn
        args += [o, wo]
        in_specs += [pl.BlockSpec((n_streams, t_len, width), blk), _const_spec(wo.shape)]
    args += [f0, norm, w_up, conv_w, conv_b, w_down]
    in_specs += [pl.BlockSpec((n_streams, FFN_CONV - 1, d_ff), per_b), _const_spec(norm.shape),
                 _const_spec(w_up.shape), _const_spec(conv_w.shape), _const_spec(conv_b.shape),
                 _const_spec(w_down.shape)]
    if out_norm is not None:
        args.append(out_norm)
        in_specs.append(_const_spec(out_norm.shape))
    kern = functools.partial(_ffn_kernel, n_streams=n_streams, t_len=t_len,
                             has_attn=attn is not None, final_norm=out_norm is not None)
    return pl.pallas_call(
        kern,
        grid=grid,
        in_specs=in_specs,
        out_specs=[
            pl.BlockSpec((n_streams, t_len, width), blk),
            pl.BlockSpec((n_streams, FFN_CONV - 1, d_ff), per_b),
        ],
        out_shape=[
            jax.ShapeDtypeStruct((bsz, seq, width), F32),
            jax.ShapeDtypeStruct((bsz, FFN_CONV - 1, d_ff), F32),
        ],
        scratch_shapes=[pltpu.VMEM((n_streams, SUBLANES, d_ff), F32)],
        compiler_params=_params(("arbitrary", "arbitrary")),
        name="ffn_attn" if attn is not None else "ffn",
    )(*args)


def _head_rms(x, g):
    outs = []
    for h in range(N_HEADS):
        outs.append(_rms(x[:, h * HEAD_DIM:(h + 1) * HEAD_DIM], g))
    return jnp.concatenate(outs, axis=-1)


def _split_heads(x):
    per_head = jnp.stack([x[:, h * HEAD_DIM:(h + 1) * HEAD_DIM] for h in range(N_HEADS)], axis=0)
    return jnp.transpose(per_head, (1, 0, 2))


def _merge_heads(x):
    per_head = jnp.transpose(x, (1, 0, 2))
    return jnp.concatenate([per_head[h] for h in range(N_HEADS)], axis=1)


def _qkv_kernel(x_ref, kvn_ref, wkv_ref, kn_ref, bn_ref, wq_ref, qn_ref,
                k_ref, v_ref, kb_ref, vb_ref, qb_ref):
    width = x_ref.shape[-1]
    x = x_ref[...]
    kv = _dot(_rms(x, kvn_ref[...]).astype(BF16), wkv_ref[...])
    k = _head_rms(kv[:, :width], kn_ref[...])
    v = kv[:, width:]
    k_ref[...] = _split_heads(k)
    v_ref[...] = _split_heads(v)
    kb_ref[...] = k.astype(BF16)
    vb_ref[...] = v.astype(BF16)
    q = _dot(_rms(x, bn_ref[...]).astype(BF16), wq_ref[...])
    qb_ref[...] = (_head_rms(q, qn_ref[...]) * SCORE_SCALE).astype(BF16)


def _qkv(x2d, kv_norm, w_kv, k_norm, b_norm, wq, q_norm, *, rows):
    n, width = x2d.shape
    blk = pl.BlockSpec((rows, width), lambda i: (i, 0))
    return pl.pallas_call(
        _qkv_kernel,
        grid=(n // rows,),
        in_specs=[blk, _const_spec(kv_norm.shape), _const_spec(w_kv.shape), _const_spec(k_norm.shape),
                  _const_spec(b_norm.shape), _const_spec(wq.shape), _const_spec(q_norm.shape)],
        out_specs=[pl.BlockSpec((rows, N_HEADS, HEAD_DIM), lambda i: (i, 0, 0))] * 2 + [blk] * 3,
        out_shape=[jax.ShapeDtypeStruct((n, N_HEADS, HEAD_DIM), F32)] * 2
        + [jax.ShapeDtypeStruct((n, width), BF16)] * 3,
        compiler_params=_params(("arbitrary",)),
        name="qkv",
    )(x2d, kv_norm, w_kv, k_norm, b_norm, wq, q_norm)


def _later_keys(n):
    j = lax.broadcasted_iota(jnp.int32, (n, n), 0)
    s = lax.broadcasted_iota(jnp.int32, (n, n), 1)
    return (j > s).astype(BF16)


def _sb_tiles(qs, kts, vts, later, carry_ref, acc_ref, mask, first, slots=None):
    n = len(qs)
    slots = list(range(n)) if slots is None else slots
    log_betas, heads_of, stays, ws = {}, {}, {}, {}
    for step in range(n + 2 * ATT_SKEW):
        u = step
        if u < n:
            y = lax.dot_general(qs[u], kts[u], (((1,), (1,)), ((), ())), preferred_element_type=F32)
            log_beta = jnp.minimum(y, 0.0) - jnp.log(1.0 + jnp.exp2(-jnp.abs(y))) * LOG2_E
            log_stay = log_beta - y
            if mask is not None:
                log_stay = jnp.where(mask, log_stay, 0.0)
            log_betas[u] = log_beta
            heads_of[u] = log_stay[:, :1]
            stays[u] = log_stay.astype(BF16)
        u = step - ATT_SKEW
        if 0 <= u < n:
            after = _dot(stays.pop(u), later)
            total = after[:, :1] + heads_of.pop(u)
            if first:
                w = jnp.exp2(log_betas.pop(u) + after)
                carry_ref[slots[u]] = total
            else:
                carry = carry_ref[slots[u]]
                w = jnp.exp2(log_betas.pop(u) + after + carry)
                carry_ref[slots[u]] = carry + total
            if mask is not None:
                w = jnp.where(mask, w, 0.0)
            ws[u] = w.astype(BF16)
        u = step - 2 * ATT_SKEW
        if 0 <= u < n:
            pv = _dot(ws.pop(u), vts[u])
            acc_ref[slots[u]] = pv if first else acc_ref[slots[u]] + pv


def _attn_prompt_kernel(q_ref, k_ref, v_ref, o_ref, carry_s, acc_s):
    i = pl.program_id(2)
    tq = q_ref.shape[1]
    n_heads = q_ref.shape[2] // HEAD_DIM
    later = _later_keys(tq)
    row = lax.broadcasted_iota(jnp.int32, (tq, tq), 0)
    col = lax.broadcasted_iota(jnp.int32, (tq, tq), 1)
    heads = [slice(h * HEAD_DIM, (h + 1) * HEAD_DIM) for h in range(n_heads)]
    qs = [q_ref[0, :, hs] for hs in heads]

    def tiles(js, mask, first):
        kts, vts = [], []
        for j in js:
            start = pl.multiple_of(j * tq, tq)
            kts += [k_ref[0, pl.ds(start, tq), hs] for hs in heads]
            vts += [v_ref[0, pl.ds(start, tq), hs] for hs in heads]
        _sb_tiles(qs * len(js), kts, vts, later, carry_s, acc_s, mask, first,
                  slots=list(range(n_heads)) * len(js))

    tiles([i], col < row, True)

    def pair(n, _):
        tiles([i - 1 - 2 * n, i - 2 - 2 * n], None, False)
        return 0

    lax.fori_loop(0, i // 2, pair, 0)

    @pl.when(i % 2 == 1)
    def _():
        tiles([0], None, False)

    for h, hs in enumerate(heads):
        o_ref[0, :, hs] = acc_s[h].astype(o_ref.dtype)


def _attn_prompt(qb, kb, vb):
    bsz, seq, width = qb.shape
    tq = min(ATT_TILE, seq)
    lanes = ATT_HEADS * HEAD_DIM
    grid = (bsz, width // lanes, seq // tq)
    q_spec = pl.BlockSpec((1, tq, lanes), lambda b, h, i: (b, i, h))
    kv_spec = pl.BlockSpec((1, seq, lanes), lambda b, h, i: (b, 0, h))
    return pl.pallas_call(
        _attn_prompt_kernel,
        grid=grid,
        in_specs=[q_spec, kv_spec, kv_spec],
        out_specs=q_spec,
        out_shape=jax.ShapeDtypeStruct((bsz, seq, width), BF16),
        scratch_shapes=[pltpu.VMEM((ATT_HEADS, tq, 1), F32), pltpu.VMEM((ATT_HEADS, tq, HEAD_DIM), F32)],
        compiler_params=_params(("arbitrary", "arbitrary", "arbitrary")),
        name="attn_prompt",
    )(qb, kb, vb)


def _attn_sample_kernel(qbd_ref, kn_ref, vn_ref, kc_ref, vc_ref, o_ref, carry_s, acc_s, *, t_q):
    j = pl.program_id(1)
    n_j = pl.num_programs(1)
    rows = qbd_ref.shape[1]
    q = qbd_ref[0]

    @pl.when(j == 0)
    def _():
        n_new = kn_ref.shape[1]
        t = lax.broadcasted_iota(jnp.int32, (rows, n_new), 0) % t_q
        s = lax.broadcasted_iota(jnp.int32, (rows, n_new), 1)
        _sb_tiles([q], [kn_ref[0]], [vn_ref[0]], _later_keys(n_new), carry_s, acc_s, s < t, True)

    @pl.when(j > 0)
    def _():
        n_tiles = kc_ref.shape[1] // ATT_TILE
        later = _later_keys(ATT_TILE)

        def tile(n, _):
            start = pl.multiple_of((n_tiles - 1 - n) * ATT_TILE, ATT_TILE)
            kt = _merge_heads(kc_ref[0, pl.ds(start, ATT_TILE)]).astype(BF16)
            vt = _merge_heads(vc_ref[0, pl.ds(start, ATT_TILE)]).astype(BF16)
            _sb_tiles([q], [kt], [vt], later, carry_s, acc_s, None, False)
            return 0

        lax.fori_loop(0, n_tiles, tile, 0)

    @pl.when(j == n_j - 1)
    def _():
        for h in range(N_HEADS):
            o_ref[0, :, h * HEAD_DIM:(h + 1) * HEAD_DIM] = acc_s[
                0, h * t_q:(h + 1) * t_q, h * HEAD_DIM:(h + 1) * HEAD_DIM].astype(o_ref.dtype)


def _attn_sample(qbd, k_new, v_new, cache_k, cache_v, *, t_q, chunk):
    bsz, rows, width = qbd.shape
    past = cache_k.shape[1]
    n_chunks = past // chunk
    n_new = k_new.shape[1]
    cache_spec = pl.BlockSpec((1, chunk, N_HEADS, HEAD_DIM),
                              lambda b, j: (b, jnp.minimum(n_chunks - j, n_chunks - 1), 0, 0))
    new_spec = pl.BlockSpec((1, n_new, width), lambda b, j: (b, 0, 0))
    kern = functools.partial(_attn_sample_kernel, t_q=t_q)
    return pl.pallas_call(
        kern,
        grid=(bsz, n_chunks + 1),
        in_specs=[pl.BlockSpec((1, rows, width), lambda b, j: (b, 0, 0)), new_spec, new_spec,
                  cache_spec, cache_spec],
        out_specs=pl.BlockSpec((1, t_q, width), lambda b, j: (b, 0, 0)),
        out_shape=jax.ShapeDtypeStruct((bsz, t_q, width), BF16),
        scratch_shapes=[pltpu.VMEM((1, rows, 1), F32), pltpu.VMEM((1, rows, width), F32)],
        compiler_params=_params(("arbitrary", "arbitrary")),
        name="attn_sample",
    )(qbd, k_new, v_new, cache_k, cache_v)


def _gate_weights(wr, wi):
    per = LRU_GROUP // wr.shape[-1]
    n_grp = wr.shape[0] // per
    eye = jnp.eye(per, dtype=wr.dtype)

    def expand(w):
        w = w.reshape(n_grp, per, w.shape[1], w.shape[2])
        return jnp.einsum('gnde,nm->gndme', w, eye).reshape(n_grp, LRU_GROUP, LRU_GROUP)

    return (0.5 * jnp.concatenate([expand(wr), expand(wi)], axis=-1)).astype(BF16)


def _block_diag_queries(qb, t_q):
    bsz = qb.shape[0]
    q = qb.reshape(bsz, t_q, N_HEADS, HEAD_DIM).transpose(0, 2, 1, 3)
    eye = jnp.eye(N_HEADS, dtype=qb.dtype)
    return jnp.einsum('bhtd,hg->bhtgd', q, eye).reshape(bsz, N_HEADS * t_q, N_HEADS * HEAD_DIM)


def _trunk(x, lru_h, lru_conv, ffn_conv, cache, w, *, n_streams, t_len, qkv_rows):
    bsz, seq, width = x.shape
    row = lambda v: v.reshape(1, -1)
    x1, h_new, c_new = _rglru(
        x, lru_h[0][:, None, :], lru_conv[0], row(w['a_norm'][0]), w['a_w_in'][0], w['a_conv_w'][0],
        row(w['a_conv_b'][0]), w['a_gate'], row(0.5 * w['a_br'][0]), row(0.5 * w['a_bi'][0]),
        row(w['a_lambda'][0]), w['a_w_out_half'][0], n_streams=n_streams, t_len=t_len)
    x2, f0_new = _ffn(x1, ffn_conv[0], row(w['f_norm'][0]), w['f_w_up'][0], w['f_conv_w'][0],
                      row(w['f_conv_b'][0]), w['f_w_down'][0], n_streams=n_streams, t_len=t_len)
    k, v, kb, vb, qb = _qkv(x2.reshape(bsz * seq, width), row(w['kv_norm']), w['w_kv'], row(w['k_norm']),
                            row(w['b_norm'][0]), w['b_wq'][0], row(w['q_norm'][0]), rows=qkv_rows)
    shape3 = (bsz, seq, width)
    kb, vb, qb = kb.reshape(shape3), vb.reshape(shape3), qb.reshape(shape3)
    if cache is None:
        o = _attn_prompt(qb, kb, vb)
    else:
        pad = ((0, 0), (0, ATT_NEW_TILE - seq), (0, 0))
        o = _attn_sample(_block_diag_queries(qb, seq), jnp.pad(kb, pad), jnp.pad(vb, pad),
                         cache[0], cache[1], t_q=seq, chunk=1024)
    y, f1_new = _ffn(x2, ffn_conv[1], row(w['f_norm'][1]), w['f_w_up'][1], w['f_conv_w'][1],
                     row(w['f_conv_b'][1]), w['f_w_down'][1], n_streams=n_streams, t_len=t_len,
                     attn=(o, w['b_wo'][0]), out_norm=row(w['out_norm']))
    heads = (bsz, seq, N_HEADS, HEAD_DIM)
    return (y, h_new.reshape(1, bsz, width), c_new[None], jnp.stack([f0_new, f1_new]),
            k.reshape(heads), v.reshape(heads))


def kernel(x_prompt, x_sample, state_lru_h, state_lru_conv, state_ffn_conv, cache_k, cache_v, a_norm, a_w_in, a_conv_w, a_conv_b, a_wr, a_br, a_wi, a_bi, a_lambda, a_w_out, kv_norm, w_kv, k_norm, b_norm, b_wq, q_norm, b_wo, f_norm, f_w_up, f_conv_w, f_conv_b, f_w_down, out_norm):
    w = dict(a_norm=a_norm, a_w_in=a_w_in.astype(BF16), a_conv_w=a_conv_w, a_conv_b=a_conv_b,
             a_gate=_gate_weights(a_wr[0], a_wi[0]), a_br=a_br, a_bi=a_bi, a_lambda=a_lambda,
             a_w_out_half=(0.5 * a_w_out).astype(BF16), kv_norm=kv_norm, w_kv=w_kv.astype(BF16), k_norm=k_norm,
             b_norm=b_norm, b_wq=b_wq.astype(BF16), q_norm=q_norm, b_wo=b_wo.astype(BF16),
             f_norm=f_norm, f_w_up=f_w_up.astype(BF16), f_conv_w=f_conv_w, f_conv_b=f_conv_b,
             f_w_down=f_w_down.astype(BF16), out_norm=out_norm)
    bp, sp, width = x_prompt.shape
    d_ff = f_w_down.shape[1]
    n_lru = a_w_in.shape[0]
    depth = f_w_up.shape[0]
    zeros = lambda *s: jnp.zeros(s, x_prompt.dtype)
    prompt = _trunk(x_prompt, zeros(n_lru, bp, width), zeros(n_lru, bp, LRU_CONV - 1, width),
                    zeros(depth, bp, FFN_CONV - 1, d_ff), None, w,
                    n_streams=1, t_len=min(512, sp), qkv_rows=min(512, bp * sp))
    bs, ss, _ = x_sample.shape
    sample = _trunk(x_sample, state_lru_h, state_lru_conv, state_ffn_conv, (cache_k, cache_v), w,
                    n_streams=bs, t_len=ss, qkv_rows=bs * ss)
    y_p, h_p, c_p, f_p, k_p, v_p = prompt
    y_s, h_s, c_s, f_s, k_s, v_s = sample
    return (y_p, y_s, h_p, c_p, f_p, k_p, v_p, h_s, c_s, f_s, k_s, v_s)
```

```python
import functools
import math

import jax
import jax.numpy as jnp
from jax import lax
from jax.experimental import pallas as pl
from jax.experimental.pallas import tpu as pltpu

N_HEADS = 8
HEAD_DIM = 128
LRU_C = 8.0
LRU_CONV = 4
FFN_CONV = 3
EPS = 1e-6

SUBLANES = 8
LANES = 128
LRU_GROUP = 256
FFN_GROUPS = (256, 768, 1024, 768, 256)
ATT_TILE = 256
ATT_HEADS = 8
ATT_NEW_TILE = 128
ATT_SKEW = 2
ROW_BLOCK = 512
CACHE_CHUNK = 1024
VMEM_LIMIT = 56 * 1024 * 1024

LOG2_E = math.log2(math.e)
SCORE_SCALE = HEAD_DIM ** -0.5 * LOG2_E

BF16 = jnp.bfloat16
F32 = jnp.float32


def _dot(a, b):
    return jnp.dot(a, b, preferred_element_type=F32)


def _rms(x, g):
    return x * lax.rsqrt(jnp.mean(x * x, axis=-1, keepdims=True) + EPS) * g


def _gelu(x):
    return 0.5 * x * (1.0 + jnp.tanh(0.7978845608028654 * (x + 0.044715 * (x * x * x))))


def _gelu2(x):
    return x * (1.0 + jnp.tanh(x * (0.7978845608028654 + (0.7978845608028654 * 0.044715) * (x * x))))


def _log_sigmoid(x):
    return jnp.minimum(x, 0.0) - jnp.log(1.0 + jnp.exp(-jnp.abs(x)))


def _const_spec(shape):
    nd = len(shape)
    return pl.BlockSpec(shape, lambda *_: (0,) * nd, pipeline_mode=pl.Buffered(1))


def _params(sem):
    return pltpu.CompilerParams(dimension_semantics=sem, vmem_limit_bytes=VMEM_LIMIT)


def _carried_conv(cur, tail_ref, w_ref, b_ref, lo, width, n_streams, t_len):
    w_grp = cur.shape[-1]
    outs = []
    for g in range(n_streams):
        x = cur[g * t_len:(g + 1) * t_len]
        xp = jnp.concatenate([tail_ref[g, :, lo:lo + w_grp], x], axis=0)
        y = b_ref[:, lo:lo + w_grp] + w_ref[width - 1:width, lo:lo + w_grp] * x
        for k in range(width - 1):
            shifted = pltpu.roll(xp, width - 1 - k, 0)[SUBLANES:]
            y = y + w_ref[k:k + 1, lo:lo + w_grp] * shifted
        tail_ref[g, :, lo:lo + w_grp] = x[t_len - SUBLANES:]
        outs.append(y)
    return outs[0] if n_streams == 1 else jnp.concatenate(outs, axis=0)


def _scan_by_groups(a, b, h_prev):
    t_len, width = a.shape
    n_grp = t_len // SUBLANES
    a = a.reshape(n_grp, SUBLANES, width)
    b = b.reshape(n_grp, SUBLANES, width)
    sub = lax.broadcasted_iota(jnp.int32, a.shape, 1)
    for d in (1, 2, 4):
        keep = sub >= d
        a_sh = pltpu.roll(a, d, 1)
        b_sh = pltpu.roll(b, d, 1)
        b = jnp.where(keep, a * b_sh + b, b)
        a = jnp.where(keep, a * a_sh, a)
    hs = []
    for k in range(n_grp):
        h = a[k] * h_prev + b[k]
        h_prev = h[SUBLANES - 1:SUBLANES]
        hs.append(h)
    return jnp.concatenate(hs, axis=0), h_prev


def _rglru_kernel(x_ref, h0_ref, c0_ref, norm_ref, win_ref, cw_ref, cb_ref, wg_ref, br_ref, bi_ref,
                  lam_ref, wout_ref, y_ref, hT_ref, cT_ref, h_s, tail_s,
                  *, n_streams, t_len):
    t = pl.program_id(1)
    n_t = pl.num_programs(1)
    width = x_ref.shape[-1]
    rows = n_streams * t_len

    @pl.when(t == 0)
    def _():
        h_s[...] = h0_ref[...]
        tail_s[:, SUBLANES - (LRU_CONV - 1):SUBLANES, :] = c0_ref[...]

    x = x_ref[...].reshape(rows, width)
    xn = _rms(x, norm_ref[...]).astype(BF16)

    def project(cg):
        lo = cg * LRU_GROUP
        return (_dot(xn, win_ref[:, lo:lo + LRU_GROUP]),
                _dot(xn, win_ref[:, width + lo:width + lo + LRU_GROUP]))

    def gates(cg, gate, rec):
        c = _carried_conv(rec, tail_s, cw_ref, cb_ref, cg * LRU_GROUP, LRU_CONV, n_streams, t_len)
        return gate, c, _dot(c.astype(BF16), wg_ref[cg])

    def recur(cg, gate, c, ri):
        lo = cg * LRU_GROUP
        r2 = jnp.tanh(ri[:, :LRU_GROUP] + br_ref[:, lo:lo + LRU_GROUP]) + 1.0
        i2 = jnp.tanh(ri[:, LRU_GROUP:] + bi_ref[:, lo:lo + LRU_GROUP]) + 1.0
        log_a = r2 * ((0.5 * LRU_C) * _log_sigmoid(lam_ref[:, lo:lo + LRU_GROUP]))
        a = jnp.exp(log_a)
        th = jnp.tanh(log_a)
        u = (-0.5 * th) * (1.0 / (1.0 - th))
        bt = jnp.where(u > 0.0, u * lax.rsqrt(u), 0.0) * i2 * c
        hs = []
        for g in range(n_streams):
            hg, h_last = _scan_by_groups(a[g * t_len:(g + 1) * t_len], bt[g * t_len:(g + 1) * t_len],
                                         h_s[g, :, lo:lo + LRU_GROUP])
            h_s[g, :, lo:lo + LRU_GROUP] = h_last
            hs.append(hg)
        h = hs[0] if n_streams == 1 else jnp.concatenate(hs, axis=0)
        return _dot((h * _gelu2(gate)).astype(BF16), wout_ref[lo:lo + LRU_GROUP, :])

    n_cg = width // LRU_GROUP
    projected = {cg: project(cg) for cg in range(min(2, n_cg))}
    gated = {0: gates(0, *projected.pop(0))}
    acc = jnp.zeros((rows, width), F32)
    for cg in range(n_cg):
        if cg + 2 < n_cg:
            projected[cg + 2] = project(cg + 2)
        if cg + 1 < n_cg:
            gated[cg + 1] = gates(cg + 1, *projected.pop(cg + 1))
        acc = acc + recur(cg, *gated.pop(cg))

    y_ref[...] = (x + acc).reshape(y_ref.shape)

    @pl.when(t == n_t - 1)
    def _():
        hT_ref[...] = h_s[...]
        cT_ref[...] = tail_s[:, SUBLANES - (LRU_CONV - 1):SUBLANES, :]


def _rglru(x, h0, c0, norm, w_in, conv_w, conv_b, w_gate, br, bi, lam, w_out, *, n_streams, t_len):
    bsz, seq, width = x.shape
    grid = (bsz // n_streams, seq // t_len)
    blk = lambda b, t: (b, t, 0)
    per_b = lambda b, t: (b, 0, 0)
    kern = functools.partial(_rglru_kernel, n_streams=n_streams, t_len=t_len)
    return pl.pallas_call(
        kern,
        grid=grid,
        in_specs=[
            pl.BlockSpec((n_streams, t_len, width), blk),
            pl.BlockSpec((n_streams, 1, width), per_b),
            pl.BlockSpec((n_streams, LRU_CONV - 1, width), per_b),
            _const_spec(norm.shape), _const_spec(w_in.shape), _const_spec(conv_w.shape),
            _const_spec(conv_b.shape), _const_spec(w_gate.shape), _const_spec(br.shape),
            _const_spec(bi.shape), _const_spec(lam.shape), _const_spec(w_out.shape),
        ],
        out_specs=[
            pl.BlockSpec((n_streams, t_len, width), blk),
            pl.BlockSpec((n_streams, 1, width), per_b),
            pl.BlockSpec((n_streams, LRU_CONV - 1, width), per_b),
        ],
        out_shape=[
            jax.ShapeDtypeStruct((bsz, seq, width), F32),
            jax.ShapeDtypeStruct((bsz, 1, width), F32),
            jax.ShapeDtypeStruct((bsz, LRU_CONV - 1, width), F32),
        ],
        scratch_shapes=[
            pltpu.VMEM((n_streams, 1, width), F32),
            pltpu.VMEM((n_streams, SUBLANES, width), F32),
        ],
        compiler_params=_params(("arbitrary", "arbitrary")),
        name="rglru",
    )(x, h0, c0, norm, w_in, conv_w, conv_b, w_gate, br, bi, lam, w_out)


def _ffn_kernel(*refs, n_streams, t_len, has_attn, final_norm):
    refs = list(refs)
    x_ref = refs.pop(0)
    o_ref = refs.pop(0) if has_attn else None
    wo_ref = refs.pop(0) if has_attn else None
    f0_ref, norm_ref, wup_ref, cw_ref, cb_ref, wdown_ref = refs[:6]
    refs = refs[6:]
    onorm_ref = refs.pop(0) if final_norm else None
    y_ref, fT_ref, tail_s = refs

    t = pl.program_id(1)
    n_t = pl.num_programs(1)
    width = x_ref.shape[-1]
    d_ff = wdown_ref.shape[0]
    rows = n_streams * t_len

    @pl.when(t == 0)
    def _():
        tail_s[:, SUBLANES - (FFN_CONV - 1):SUBLANES, :] = f0_ref[...]

    x = x_ref[...].reshape(rows, width)
    if has_attn:
        x = x + _dot(o_ref[...].reshape(rows, width), wo_ref[...])
    xn = _rms(x, norm_ref[...]).astype(BF16)

    bounds = [0]
    for g in FFN_GROUPS:
        bounds.append(bounds[-1] + g)

    def up(fg):
        lo, hi = bounds[fg], bounds[fg + 1]
        return (_dot(xn, wup_ref[:, lo:hi]), _dot(xn, wup_ref[:, d_ff + lo:d_ff + hi]))

    n_grp = len(FFN_GROUPS)
    acc = jnp.zeros((rows, width), F32)
    nxt = up(0)
    for fg in range(n_grp):
        lo, hi = bounds[fg], bounds[fg + 1]
        gpre, u = nxt
        if fg + 1 < n_grp:
            nxt = up(fg + 1)
        gc = _carried_conv(gpre, tail_s, cw_ref, cb_ref, lo, FFN_CONV, n_streams, t_len)
        acc = acc + _dot((_gelu(gc) * u).astype(BF16), wdown_ref[lo:hi, :])

    y = x + acc
    if final_norm:
        y = _rms(y, onorm_ref[...])
    y_ref[...] = y.reshape(y_ref.shape)

    @pl.when(t == n_t - 1)
    def _():
        fT_ref[...] = tail_s[:, SUBLANES - (FFN_CONV - 1):SUBLANES, :]


def _ffn(x, f0, norm, w_up, conv_w, conv_b, w_down, *, n_streams, t_len, attn=None, out_norm=None):
    bsz, seq, width = x.shape
    d_ff = w_down.shape[0]
    assert sum(FFN_GROUPS) == d_ff
    grid = (bsz // n_streams, seq // t_len)
    blk = lambda b, t: (b, t, 0)
    per_b = lambda b, t: (b, 0, 0)
    args = [x]
    in_specs = [pl.BlockSpec((n_streams, t_len, width), blk)]
    if attn is not None:
        o, wo = attn
        args += [o, wo]
        in_specs += [pl.BlockSpec((n_streams, t_len, width), blk), _const_spec(wo.shape)]
    args += [f0, norm, w_up, conv_w, conv_b, w_down]
    in_specs += [pl.BlockSpec((n_streams, FFN_CONV - 1, d_ff), per_b), _const_spec(norm.shape),
                 _const_spec(w_up.shape), _const_spec(conv_w.shape), _const_spec(conv_b.shape),
                 _const_spec(w_down.shape)]
    if out_norm is not None:
        args.append(out_norm)
        in_specs.append(_const_spec(out_norm.shape))
    kern = functools.partial(_ffn_kernel, n_streams=n_streams, t_len=t_len,
                             has_attn=attn is not None, final_norm=out_norm is not None)
    return pl.pallas_call(
        kern,
        grid=grid,
        in_specs=in_specs,
        out_specs=[
            pl.BlockSpec((n_streams, t_len, width), blk),
            pl.BlockSpec((n_streams, FFN_CONV - 1, d_ff), per_b),
        ],
        out_shape=[
            jax.ShapeDtypeStruct((bsz, seq, width), F32),
            jax.ShapeDtypeStruct((bsz, FFN_CONV - 1, d_ff), F32),
        ],
        scratch_shapes=[pltpu.VMEM((n_streams, SUBLANES, d_ff), F32)],
        compiler_params=_params(("arbitrary", "arbitrary")),
        name="ffn_attn" if attn is not None else "ffn",
    )(*args)


def _head_rms(x, g):
    outs = []
    for h in range(N_HEADS):
        outs.append(_rms(x[:, h * HEAD_DIM:(h + 1) * HEAD_DIM], g))
    return jnp.concatenate(outs, axis=-1)


def _split_heads(x):
    per_head = jnp.stack([x[:, h * HEAD_DIM:(h + 1) * HEAD_DIM] for h in range(N_HEADS)], axis=0)
    return jnp.transpose(per_head, (1, 0, 2))


def _merge_heads(x):
    per_head = jnp.transpose(x, (1, 0, 2))
    return jnp.concatenate([per_head[h] for h in range(N_HEADS)], axis=1)


def _qkv_kernel(x_ref, kvn_ref, wkv_ref, kn_ref, bn_ref, wq_ref, qn_ref,
                k_ref, v_ref, kb_ref, vb_ref, qb_ref):
    width = x_ref.shape[-1]
    x = x_ref[...]
    x = x * lax.rsqrt(jnp.mean(x * x, axis=-1, keepdims=True) + EPS)
    kv = _dot((x * kvn_ref[...]).astype(BF16), wkv_ref[...])
    k = _head_rms(kv[:, :width], kn_ref[...])
    v = kv[:, width:]
    k_ref[...] = _split_heads(k)
    v_ref[...] = _split_heads(v)
    kb_ref[...] = k.astype(BF16)
    vb_ref[...] = v.astype(BF16)
    q = _dot((x * bn_ref[...]).astype(BF16), wq_ref[...])
    qb_ref[...] = (_head_rms(q, qn_ref[...]) * SCORE_SCALE).astype(BF16)


def _qkv(x2d, kv_norm, w_kv, k_norm, b_norm, wq, q_norm, *, rows):
    n, width = x2d.shape
    blk = pl.BlockSpec((rows, width), lambda i: (i, 0))
    return pl.pallas_call(
        _qkv_kernel,
        grid=(n // rows,),
        in_specs=[blk, _const_spec(kv_norm.shape), _const_spec(w_kv.shape), _const_spec(k_norm.shape),
                  _const_spec(b_norm.shape), _const_spec(wq.shape), _const_spec(q_norm.shape)],
        out_specs=[pl.BlockSpec((rows, N_HEADS, HEAD_DIM), lambda i: (i, 0, 0))] * 2 + [blk] * 3,
        out_shape=[jax.ShapeDtypeStruct((n, N_HEADS, HEAD_DIM), F32)] * 2
        + [jax.ShapeDtypeStruct((n, width), BF16)] * 3,
        compiler_params=_params(("arbitrary",)),
        name="qkv",
    )(x2d, kv_norm, w_kv, k_norm, b_norm, wq, q_norm)


def _later_keys(n):
    j = lax.broadcasted_iota(jnp.int32, (n, n + LANES), 0)
    s = lax.broadcasted_iota(jnp.int32, (n, n + LANES), 1)
    return ((j > s) | (s >= n)).astype(BF16)


def _sb_tiles(qs, kts, vts, later, carry_ref, acc_ref, masks, firsts, slots=None):
    n = len(qs)
    slots = list(range(n)) if slots is None else slots
    log_betas, stays, ws = {}, {}, {}
    for step in range(n + 2 * ATT_SKEW):
        u = step
        if u < n:
            mask = masks[u]
            y = lax.dot_general(qs[u], kts[u], (((1,), (1,)), ((), ())), preferred_element_type=F32)
            log_beta = jnp.minimum(y, 0.0) - jnp.log(1.0 + jnp.exp2(-jnp.abs(y))) * LOG2_E
            log_stay = log_beta - y
            if mask is not None:
                log_stay = jnp.where(mask, log_stay, 0.0)
            log_betas[u] = log_beta
            stays[u] = log_stay.astype(BF16)
        u = step - ATT_SKEW
        if 0 <= u < n:
            mask = masks[u]
            sums = _dot(stays.pop(u), later)
            n_keys = sums.shape[1] - LANES
            after, total = sums[:, :n_keys], sums[:, n_keys:]
            if firsts[u]:
                w = jnp.exp2(log_betas.pop(u) + after)
                carry_ref[slots[u]] = total
            else:
                carry = carry_ref[slots[u]]
                w = jnp.exp2(log_betas.pop(u) + after + jnp.concatenate([carry] * (n_keys // LANES), axis=1))
                carry_ref[slots[u]] = carry + total
            if mask is not None:
                w = jnp.where(mask, w, 0.0)
            ws[u] = w.astype(BF16)
        u = step - 2 * ATT_SKEW
        if 0 <= u < n:
            pv = _dot(ws.pop(u), vts[u])
            acc_ref[slots[u]] = pv if firsts[u] else acc_ref[slots[u]] + pv


def _attn_prompt_kernel(q_ref, k_ref, v_ref, o_ref, carry_s, acc_s):
    i = pl.program_id(2)
    tq = q_ref.shape[1]
    n_heads = q_ref.shape[2] // HEAD_DIM
    later = _later_keys(tq)
    row = lax.broadcasted_iota(jnp.int32, (tq, tq), 0)
    col = lax.broadcasted_iota(jnp.int32, (tq, tq), 1)
    heads = [slice(h * HEAD_DIM, (h + 1) * HEAD_DIM) for h in range(n_heads)]
    qs = [q_ref[0, :, hs] for hs in heads]

    def tiles(js, diagonal_first):
        kts, vts, masks, firsts = [], [], [], []
        for n, j in enumerate(js):
            start = pl.multiple_of(j * tq, tq)
            kts += [k_ref[0, pl.ds(start, tq), hs] for hs in heads]
            vts += [v_ref[0, pl.ds(start, tq), hs] for hs in heads]
            on_diagonal = diagonal_first and n == 0
            masks += [col < row if on_diagonal else None] * n_heads
            firsts += [on_diagonal] * n_heads
        _sb_tiles(qs * len(js), kts, vts, later, carry_s, acc_s, masks, firsts,
                  slots=list(range(n_heads)) * len(js))

    @pl.when(i == 0)
    def _():
        tiles([0], True)

    @pl.when(i > 0)
    def _():
        tiles([i, i - 1], True)

    rest = jnp.maximum(i - 1, 0)

    def pair(n, _):
        tiles([rest - 1 - 2 * n, rest - 2 - 2 * n], False)
        return 0

    lax.fori_loop(0, rest // 2, pair, 0)

    @pl.when(rest % 2 == 1)
    def _():
        tiles([0], False)

    for h, hs in enumerate(heads):
        o_ref[0, :, hs] = acc_s[h].astype(o_ref.dtype)


def _attn_prompt(qb, kb, vb):
    bsz, seq, width = qb.shape
    tq = min(ATT_TILE, seq)
    lanes = ATT_HEADS * HEAD_DIM
    grid = (bsz, width // lanes, seq // tq)
    q_spec = pl.BlockSpec((1, tq, lanes), lambda b, h, i: (b, i, h))
    kv_spec = pl.BlockSpec((1, seq, lanes), lambda b, h, i: (b, 0, h))
    return pl.pallas_call(
        _attn_prompt_kernel,
        grid=grid,
        in_specs=[q_spec, kv_spec, kv_spec],
        out_specs=q_spec,
        out_shape=jax.ShapeDtypeStruct((bsz, seq, width), BF16),
        scratch_shapes=[pltpu.VMEM((ATT_HEADS, tq, LANES), F32), pltpu.VMEM((ATT_HEADS, tq, HEAD_DIM), F32)],
        compiler_params=_params(("arbitrary", "arbitrary", "arbitrary")),
        name="attn_prompt",
    )(qb, kb, vb)


def _attn_sample_kernel(qbd_ref, kn_ref, vn_ref, kc_ref, vc_ref, o_ref, carry_s, acc_s, *, t_q):
    j = pl.program_id(1)
    n_j = pl.num_programs(1)
    rows = qbd_ref.shape[1]
    q = qbd_ref[0]

    @pl.when(j == 0)
    def _():
        n_new = kn_ref.shape[1]
        t = lax.broadcasted_iota(jnp.int32, (rows, n_new), 0) % t_q
        s = lax.broadcasted_iota(jnp.int32, (rows, n_new), 1)
        _sb_tiles([q], [kn_ref[0]], [vn_ref[0]], _later_keys(n_new), carry_s, acc_s, [s < t], [True])

    @pl.when(j > 0)
    def _():
        n_tiles = kc_ref.shape[1] // ATT_TILE
        later = _later_keys(ATT_TILE)

        def tile(n, _):
            start = pl.multiple_of((n_tiles - 1 - n) * ATT_TILE, ATT_TILE)
            kt = _merge_heads(kc_ref[0, pl.ds(start, ATT_TILE)]).astype(BF16)
            vt = _merge_heads(vc_ref[0, pl.ds(start, ATT_TILE)]).astype(BF16)
            _sb_tiles([q], [kt], [vt], later, carry_s, acc_s, [None], [False])
            return 0

        lax.fori_loop(0, n_tiles, tile, 0)

    @pl.when(j == n_j - 1)
    def _():
        for h in range(N_HEADS):
            o_ref[0, :, h * HEAD_DIM:(h + 1) * HEAD_DIM] = acc_s[
                0, h * t_q:(h + 1) * t_q, h * HEAD_DIM:(h + 1) * HEAD_DIM].astype(o_ref.dtype)


def _attn_sample(qbd, k_new, v_new, cache_k, cache_v, *, t_q, chunk):
    bsz, rows, width = qbd.shape
    past = cache_k.shape[1]
    n_chunks = past // chunk
    n_new = k_new.shape[1]
    cache_spec = pl.BlockSpec((1, chunk, N_HEADS, HEAD_DIM),
                              lambda b, j: (b, jnp.minimum(n_chunks - j, n_chunks - 1), 0, 0))
    new_spec = pl.BlockSpec((1, n_new, width), lambda b, j: (b, 0, 0))
    kern = functools.partial(_attn_sample_kernel, t_q=t_q)
    return pl.pallas_call(
        kern,
        grid=(bsz, n_chunks + 1),
        in_specs=[pl.BlockSpec((1, rows, width), lambda b, j: (b, 0, 0)), new_spec, new_spec,
                  cache_spec, cache_spec],
        out_specs=pl.BlockSpec((1, t_q, width), lambda b, j: (b, 0, 0)),
        out_shape=jax.ShapeDtypeStruct((bsz, t_q, width), BF16),
        scratch_shapes=[pltpu.VMEM((1, rows, LANES), F32), pltpu.VMEM((1, rows, width), F32)],
        compiler_params=_params(("arbitrary", "arbitrary")),
        name="attn_sample",
    )(qbd, k_new, v_new, cache_k, cache_v)


def _gate_weights(wr, wi):
    per = LRU_GROUP // wr.shape[-1]
    n_grp = wr.shape[0] // per
    eye = jnp.eye(per, dtype=wr.dtype)

    def expand(w):
        w = w.reshape(n_grp, per, w.shape[1], w.shape[2])
        return jnp.einsum('gnde,nm->gndme', w, eye).reshape(n_grp, LRU_GROUP, LRU_GROUP)

    return (0.5 * jnp.concatenate([expand(wr), expand(wi)], axis=-1)).astype(BF16)


def _block_diag_queries(qb, t_q):
    bsz = qb.shape[0]
    q = qb.reshape(bsz, t_q, N_HEADS, HEAD_DIM).transpose(0, 2, 1, 3)
    eye = jnp.eye(N_HEADS, dtype=qb.dtype)
    return jnp.einsum('bhtd,hg->bhtgd', q, eye).reshape(bsz, N_HEADS * t_q, N_HEADS * HEAD_DIM)


def _trunk(x, lru_h, lru_conv, ffn_conv, cache, w, *, n_streams, t_len, qkv_rows):
    bsz, seq, width = x.shape
    row = lambda v: v.reshape(1, -1)
    x1, h_new, c_new = _rglru(
        x, lru_h[0][:, None, :], lru_conv[0], row(w['a_norm'][0]), w['a_w_in'][0], w['a_conv_w'][0],
        row(w['a_conv_b'][0]), w['a_gate'], row(0.5 * w['a_br'][0]), row(0.5 * w['a_bi'][0]),
        row(w['a_lambda'][0]), w['a_w_out_half'][0], n_streams=n_streams, t_len=t_len)
    x2, f0_new = _ffn(x1, ffn_conv[0], row(w['f_norm'][0]), w['f_w_up'][0], w['f_conv_w'][0],
                      row(w['f_conv_b'][0]), w['f_w_down'][0], n_streams=n_streams, t_len=t_len)
    k, v, kb, vb, qb = _qkv(x2.reshape(bsz * seq, width), row(w['kv_norm']), w['w_kv'], row(w['k_norm']),
                            row(w['b_norm'][0]), w['b_wq'][0], row(w['q_norm'][0]), rows=qkv_rows)
    shape3 = (bsz, seq, width)
    kb, vb, qb = kb.reshape(shape3), vb.reshape(shape3), qb.reshape(shape3)
    if cache is None:
        o = _attn_prompt(qb, kb, vb)
    else:
        pad = ((0, 0), (0, ATT_NEW_TILE - seq), (0, 0))
        o = _attn_sample(_block_diag_queries(qb, seq), jnp.pad(kb, pad), jnp.pad(vb, pad),
                         cache[0], cache[1], t_q=seq, chunk=CACHE_CHUNK)
    y, f1_new = _ffn(x2, ffn_conv[1], row(w['f_norm'][1]), w['f_w_up'][1], w['f_conv_w'][1],
                     row(w['f_conv_b'][1]), w['f_w_down'][1], n_streams=n_streams, t_len=t_len,
                     attn=(o, w['b_wo'][0]), out_norm=row(w['out_norm']))
    heads = (bsz, seq, N_HEADS, HEAD_DIM)
    return (y, h_new.reshape(1, bsz, width), c_new[None], jnp.stack([f0_new, f1_new]),
            k.reshape(heads), v.reshape(heads))


def kernel(x_prompt, x_sample, state_lru_h, state_lru_conv, state_ffn_conv, cache_k, cache_v, a_norm, a_w_in, a_conv_w, a_conv_b, a_wr, a_br, a_wi, a_bi, a_lambda, a_w_out, kv_norm, w_kv, k_norm, b_norm, b_wq, q_norm, b_wo, f_norm, f_w_up, f_conv_w, f_conv_b, f_w_down, out_norm):
    w = dict(a_norm=a_norm, a_w_in=a_w_in.astype(BF16), a_conv_w=a_conv_w, a_conv_b=a_conv_b,
             a_gate=_gate_weights(a_wr[0], a_wi[0]), a_br=a_br, a_bi=a_bi, a_lambda=a_lambda,
             a_w_out_half=(0.5 * a_w_out).astype(BF16), kv_norm=kv_norm, w_kv=w_kv.astype(BF16), k_norm=k_norm,
             b_norm=b_norm, b_wq=b_wq.astype(BF16), q_norm=q_norm, b_wo=b_wo.astype(BF16),
             f_norm=f_norm, f_w_up=f_w_up.astype(BF16), f_conv_w=f_conv_w, f_conv_b=f_conv_b,
             f_w_down=f_w_down.astype(BF16), out_norm=out_norm)
    bp, sp, width = x_prompt.shape
    d_ff = f_w_down.shape[1]
    n_lru = a_w_in.shape[0]
    depth = f_w_up.shape[0]
    zeros = lambda *s: jnp.zeros(s, x_prompt.dtype)
    prompt = _trunk(x_prompt, zeros(n_lru, bp, width), zeros(n_lru, bp, LRU_CONV - 1, width),
                    zeros(depth, bp, FFN_CONV - 1, d_ff), None, w,
                    n_streams=1, t_len=min(ROW_BLOCK, sp), qkv_rows=min(ROW_BLOCK, bp * sp))
    bs, ss, _ = x_sample.shape
    sample = _trunk(x_sample, state_lru_h, state_lru_conv, state_ffn_conv, (cache_k, cache_v), w,
                    n_streams=bs, t_len=ss, qkv_rows=bs * ss)
    y_p, h_p, c_p, f_p, k_p, v_p = prompt
    y_s, h_s, c_s, f_s, k_s, v_s = sample
    return (y_p, y_s, h_p, c_p, f_p, k_p, v_p, h_s, c_s, f_s, k_s, v_s)
```

```python
import functools
import math

import jax
import jax.numpy as jnp
from jax import lax
from jax.experimental import pallas as pl
from jax.experimental.pallas import tpu as pltpu

N_HEADS = 8
HEAD_DIM = 128
LRU_C = 8.0
LRU_CONV = 4
FFN_CONV = 3
EPS = 1e-6

SUBLANES = 8
LANES = 128
LRU_GROUP = 256
FFN_GROUPS = (256, 512, 1024, 1024, 256)
ATT_TILE = 256
ATT_HEADS = 8
ATT_NEW_TILE = 128
ATT_SKEW = 2
ROW_BLOCK = 512
QKV_ROWS = 1024
CACHE_CHUNK = 1024
VMEM_LIMIT = 56 * 1024 * 1024

LOG2_E = math.log2(math.e)
SCORE_SCALE = HEAD_DIM ** -0.5 * LOG2_E

BF16 = jnp.bfloat16
F32 = jnp.float32


def _dot(a, b):
    return jnp.dot(a, b, preferred_element_type=F32)


def _rms(x, g):
    return x * lax.rsqrt(jnp.mean(x * x, axis=-1, keepdims=True) + EPS) * g


def _gelu(x):
    return 0.5 * x * (1.0 + jnp.tanh(0.7978845608028654 * (x + 0.044715 * (x * x * x))))


def _gelu2(x):
    return x * (1.0 + jnp.tanh(x * (0.7978845608028654 + (0.7978845608028654 * 0.044715) * (x * x))))


def _log_sigmoid(x):
    return jnp.minimum(x, 0.0) - jnp.log(1.0 + jnp.exp(-jnp.abs(x)))


def _const_spec(shape):
    nd = len(shape)
    return pl.BlockSpec(shape, lambda *_: (0,) * nd, pipeline_mode=pl.Buffered(1))


def _params(sem):
    return pltpu.CompilerParams(dimension_semantics=sem, vmem_limit_bytes=VMEM_LIMIT)


def _carried_conv(cur, tail_ref, w_ref, b_ref, lo, width, n_streams, t_len):
    w_grp = cur.shape[-1]
    outs = []
    for g in range(n_streams):
        x = cur[g * t_len:(g + 1) * t_len]
        xp = jnp.concatenate([tail_ref[g, :, lo:lo + w_grp], x], axis=0)
        y = b_ref[:, lo:lo + w_grp] + w_ref[width - 1:width, lo:lo + w_grp] * x
        for k in range(width - 1):
            shifted = pltpu.roll(xp, width - 1 - k, 0)[SUBLANES:]
            y = y + w_ref[k:k + 1, lo:lo + w_grp] * shifted
        tail_ref[g, :, lo:lo + w_grp] = x[t_len - SUBLANES:]
        outs.append(y)
    return outs[0] if n_streams == 1 else jnp.concatenate(outs, axis=0)


def _scan_by_groups(a, b, h_prev):
    t_len, width = a.shape
    n_grp = t_len // SUBLANES
    a = a.reshape(n_grp, SUBLANES, width)
    b = b.reshape(n_grp, SUBLANES, width)
    sub = lax.broadcasted_iota(jnp.int32, a.shape, 1)
    for d in (1, 2, 4):
        keep = sub >= d
        a_sh = pltpu.roll(a, d, 1)
        b_sh = pltpu.roll(b, d, 1)
        b = jnp.where(keep, a * b_sh + b, b)
        a = jnp.where(keep, a * a_sh, a)
    hs = []
    for k in range(n_grp):
        h = a[k] * h_prev + b[k]
        h_prev = h[SUBLANES - 1:SUBLANES]
        hs.append(h)
    return jnp.concatenate(hs, axis=0), h_prev


def _rglru_kernel(x_ref, h0_ref, c0_ref, norm_ref, win_ref, cw_ref, cb_ref, wg_ref, br_ref, bi_ref,
                  lam_ref, wout_ref, y_ref, hT_ref, cT_ref, h_s, tail_s,
                  *, n_streams, t_len):
    t = pl.program_id(1)
    n_t = pl.num_programs(1)
    width = x_ref.shape[-1]
    rows = n_streams * t_len

    @pl.when(t == 0)
    def _():
        h_s[...] = h0_ref[...]
        tail_s[:, SUBLANES - (LRU_CONV - 1):SUBLANES, :] = c0_ref[...]

    x = x_ref[...].reshape(rows, width)
    xn = _rms(x, norm_ref[...]).astype(BF16)

    def project(cg):
        lo = cg * LRU_GROUP
        return (_dot(xn, win_ref[:, lo:lo + LRU_GROUP]),
                _dot(xn, win_ref[:, width + lo:width + lo + LRU_GROUP]))

    def gates(cg, gate, rec):
        c = _carried_conv(rec, tail_s, cw_ref, cb_ref, cg * LRU_GROUP, LRU_CONV, n_streams, t_len)
        return gate, c, _dot(c.astype(BF16), wg_ref[cg])

    def recur(cg, gate, c, ri):
        lo = cg * LRU_GROUP
        r2 = jnp.tanh(ri[:, :LRU_GROUP] + br_ref[:, lo:lo + LRU_GROUP]) + 1.0
        i2 = jnp.tanh(ri[:, LRU_GROUP:] + bi_ref[:, lo:lo + LRU_GROUP]) + 1.0
        log_a = r2 * ((0.5 * LRU_C) * _log_sigmoid(lam_ref[:, lo:lo + LRU_GROUP]))
        a = jnp.exp(log_a)
        th = jnp.tanh(log_a)
        u = (-0.5 * th) * (1.0 / (1.0 - th))
        bt = jnp.where(u > 0.0, u * lax.rsqrt(u), 0.0) * i2 * c
        hs = []
        for g in range(n_streams):
            hg, h_last = _scan_by_groups(a[g * t_len:(g + 1) * t_len], bt[g * t_len:(g + 1) * t_len],
                                         h_s[g, :, lo:lo + LRU_GROUP])
            h_s[g, :, lo:lo + LRU_GROUP] = h_last
            hs.append(hg)
        h = hs[0] if n_streams == 1 else jnp.concatenate(hs, axis=0)
        return _dot((h * _gelu2(gate)).astype(BF16), wout_ref[lo:lo + LRU_GROUP, :])

    n_cg = width // LRU_GROUP
    projected = {cg: project(cg) for cg in range(min(2, n_cg))}
    gated = {0: gates(0, *projected.pop(0))}
    acc = jnp.zeros((rows, width), F32)
    for cg in range(n_cg):
        if cg + 2 < n_cg:
            projected[cg + 2] = project(cg + 2)
        if cg + 1 < n_cg:
            gated[cg + 1] = gates(cg + 1, *projected.pop(cg + 1))
        acc = acc + recur(cg, *gated.pop(cg))

    y_ref[...] = (x + acc).reshape(y_ref.shape)

    @pl.when(t == n_t - 1)
    def _():
        hT_ref[...] = h_s[...]
        cT_ref[...] = tail_s[:, SUBLANES - (LRU_CONV - 1):SUBLANES, :]


def _rglru(x, h0, c0, norm, w_in, conv_w, conv_b, w_gate, br, bi, lam, w_out, *, n_streams, t_len):
    bsz, seq, width = x.shape
    grid = (bsz // n_streams, seq // t_len)
    blk = lambda b, t: (b, t, 0)
    per_b = lambda b, t: (b, 0, 0)
    kern = functools.partial(_rglru_kernel, n_streams=n_streams, t_len=t_len)
    return pl.pallas_call(
        kern,
        grid=grid,
        in_specs=[
            pl.BlockSpec((n_streams, t_len, width), blk),
            pl.BlockSpec((n_streams, 1, width), per_b),
            pl.BlockSpec((n_streams, LRU_CONV - 1, width), per_b),
            _const_spec(norm.shape), _const_spec(w_in.shape), _const_spec(conv_w.shape),
            _const_spec(conv_b.shape), _const_spec(w_gate.shape), _const_spec(br.shape),
            _const_spec(bi.shape), _const_spec(lam.shape), _const_spec(w_out.shape),
        ],
        out_specs=[
            pl.BlockSpec((n_streams, t_len, width), blk),
            pl.BlockSpec((n_streams, 1, width), per_b),
            pl.BlockSpec((n_streams, LRU_CONV - 1, width), per_b),
        ],
        out_shape=[
            jax.ShapeDtypeStruct((bsz, seq, width), F32),
            jax.ShapeDtypeStruct((bsz, 1, width), F32),
            jax.ShapeDtypeStruct((bsz, LRU_CONV - 1, width), F32),
        ],
        scratch_shapes=[
            pltpu.VMEM((n_streams, 1, width), F32),
            pltpu.VMEM((n_streams, SUBLANES, width), F32),
        ],
        compiler_params=_params(("arbitrary", "arbitrary")),
        name="rglru",
    )(x, h0, c0, norm, w_in, conv_w, conv_b, w_gate, br, bi, lam, w_out)


def _ffn_kernel(*refs, n_streams, t_len, has_attn, final_norm):
    refs = list(refs)
    x_ref = refs.pop(0)
    o_ref = refs.pop(0) if has_attn else None
    wo_ref = refs.pop(0) if has_attn else None
    f0_ref, norm_ref, wup_ref, cw_ref, cb_ref, wdown_ref = refs[:6]
    refs = refs[6:]
    onorm_ref = refs.pop(0) if final_norm else None
    y_ref, fT_ref, tail_s = refs

    t = pl.program_id(1)
    n_t = pl.num_programs(1)
    width = x_ref.shape[-1]
    d_ff = wdown_ref.shape[0]
    rows = n_streams * t_len

    @pl.when(t == 0)
    def _():
        tail_s[:, SUBLANES - (FFN_CONV - 1):SUBLANES, :] = f0_ref[...]

    x = x_ref[...].reshape(rows, width)
    if has_attn:
        x = x + _dot(o_ref[...].reshape(rows, width), wo_ref[...])
    xn = _rms(x, norm_ref[...]).astype(BF16)

    bounds = [0]
    for g in FFN_GROUPS:
        bounds.append(bounds[-1] + g)

    def up(fg):
        lo, hi = bounds[fg], bounds[fg + 1]
        return (_dot(xn, wup_ref[:, lo:hi]), _dot(xn, wup_ref[:, d_ff + lo:d_ff + hi]))

    n_grp = len(FFN_GROUPS)
    acc = jnp.zeros((rows, width), F32)
    nxt = up(0)
    for fg in range(n_grp):
        lo, hi = bounds[fg], bounds[fg + 1]
        gpre, u = nxt
        if fg + 1 < n_grp:
            nxt = up(fg + 1)
        gc = _carried_conv(gpre, tail_s, cw_ref, cb_ref, lo, FFN_CONV, n_streams, t_len)
        acc = acc + _dot((_gelu(gc) * u).astype(BF16), wdown_ref[lo:hi, :])

    y = x + acc
    if final_norm:
        y = _rms(y, onorm_ref[...])
    y_ref[...] = y.reshape(y_ref.shape)

    @pl.when(t == n_t - 1)
    def _():
        fT_ref[...] = tail_s[:, SUBLANES - (FFN_CONV - 1):SUBLANES, :]


def _ffn(x, f0, norm, w_up, conv_w, conv_b, w_down, *, n_streams, t_len, attn=None, out_norm=None):
    bsz, seq, width = x.shape
    d_ff = w_down.shape[0]
    assert sum(FFN_GROUPS) == d_ff
    grid = (bsz // n_streams, seq // t_len)
    blk = lambda b, t: (b, t, 0)
    per_b = lambda b, t: (b, 0, 0)
    args = [x]
    in_specs = [pl.BlockSpec((n_streams, t_len, width), blk)]
    if attn is not None:
        o, wo = attn
        args += [o, wo]
        in_specs += [pl.BlockSpec((n_streams, t_len, width), blk), _const_spec(wo.shape)]
    args += [f0, norm, w_up, conv_w, conv_b, w_down]
    in_specs += [pl.BlockSpec((n_streams, FFN_CONV - 1, d_ff), per_b), _const_spec(norm.shape),
                 _const_spec(w_up.shape), _const_spec(conv_w.shape), _const_spec(conv_b.shape),
                 _const_spec(w_down.shape)]
    if out_norm is not None:
        args.append(out_norm)
        in_specs.append(_const_spec(out_norm.shape))
    kern = functools.partial(_ffn_kernel, n_streams=n_streams, t_len=t_len,
                             has_attn=attn is not None, final_norm=out_norm is not None)
    return pl.pallas_call(
        kern,
        grid=grid,
        in_specs=in_specs,
        out_specs=[
            pl.BlockSpec((n_streams, t_len, width), blk),
            pl.BlockSpec((n_streams, FFN_CONV - 1, d_ff), per_b),
        ],
        out_shape=[
            jax.ShapeDtypeStruct((bsz, seq, width), F32),
            jax.ShapeDtypeStruct((bsz, FFN_CONV - 1, d_ff), F32),
        ],
        scratch_shapes=[pltpu.VMEM((n_streams, SUBLANES, d_ff), F32)],
        compiler_params=_params(("arbitrary", "arbitrary")),
        name="ffn_attn" if attn is not None else "ffn",
    )(*args)


def _head_rms(x, g):
    outs = []
    for h in range(N_HEADS):
        outs.append(_rms(x[:, h * HEAD_DIM:(h + 1) * HEAD_DIM], g))
    return jnp.concatenate(outs, axis=-1)


def _split_heads(x):
    per_head = jnp.stack([x[:, h * HEAD_DIM:(h + 1) * HEAD_DIM] for h in range(N_HEADS)], axis=0)
    return jnp.transpose(per_head, (1, 0, 2))


def _merge_heads(x):
    per_head = jnp.transpose(x, (1, 0, 2))
    return jnp.concatenate([per_head[h] for h in range(N_HEADS)], axis=1)


def _qkv_kernel(x_ref, kvn_ref, wkv_ref, kn_ref, bn_ref, wq_ref, qn_ref,
                k_ref, v_ref, kb_ref, vb_ref, qb_ref):
    width = x_ref.shape[-1]
    x = x_ref[...]
    x = x * lax.rsqrt(jnp.mean(x * x, axis=-1, keepdims=True) + EPS)
    kv = _dot((x * kvn_ref[...]).astype(BF16), wkv_ref[...])
    k = _head_rms(kv[:, :width], kn_ref[...])
    v = kv[:, width:]
    k_ref[...] = _split_heads(k)
    v_ref[...] = _split_heads(v)
    kb_ref[...] = k.astype(BF16)
    vb_ref[...] = v.astype(BF16)
    q = _dot((x * bn_ref[...]).astype(BF16), wq_ref[...])
    qb_ref[...] = (_head_rms(q, qn_ref[...]) * SCORE_SCALE).astype(BF16)


def _qkv(x2d, kv_norm, w_kv, k_norm, b_norm, wq, q_norm, *, rows):
    n, width = x2d.shape
    blk = pl.BlockSpec((rows, width), lambda i: (i, 0))
    return pl.pallas_call(
        _qkv_kernel,
        grid=(n // rows,),
        in_specs=[blk, _const_spec(kv_norm.shape), _const_spec(w_kv.shape), _const_spec(k_norm.shape),
                  _const_spec(b_norm.shape), _const_spec(wq.shape), _const_spec(q_norm.shape)],
        out_specs=[pl.BlockSpec((rows, N_HEADS, HEAD_DIM), lambda i: (i, 0, 0))] * 2 + [blk] * 3,
        out_shape=[jax.ShapeDtypeStruct((n, N_HEADS, HEAD_DIM), F32)] * 2
        + [jax.ShapeDtypeStruct((n, width), BF16)] * 3,
        compiler_params=_params(("arbitrary",)),
        name="qkv",
    )(x2d, kv_norm, w_kv, k_norm, b_norm, wq, q_norm)


def _later_keys(n):
    j = lax.broadcasted_iota(jnp.int32, (n, n + LANES), 0)
    s = lax.broadcasted_iota(jnp.int32, (n, n + LANES), 1)
    return ((j > s) | (s >= n)).astype(BF16)


def _sb_tiles(qs, kts, vts, later, carry_ref, acc_ref, masks, firsts, slots=None):
    n = len(qs)
    slots = list(range(n)) if slots is None else slots
    log_betas, stays, ws = {}, {}, {}
    for step in range(n + 2 * ATT_SKEW):
        u = step
        if u < n:
            mask = masks[u]
            y = lax.dot_general(qs[u], kts[u], (((1,), (1,)), ((), ())), preferred_element_type=F32)
            log_beta = jnp.minimum(y, 0.0) - jnp.log(1.0 + jnp.exp2(-jnp.abs(y))) * LOG2_E
            log_stay = log_beta - y
            if mask is not None:
                log_stay = jnp.where(mask, log_stay, 0.0)
            log_betas[u] = log_beta
            stays[u] = log_stay.astype(BF16)
        u = step - ATT_SKEW
        if 0 <= u < n:
            mask = masks[u]
            sums = _dot(stays.pop(u), later)
            n_keys = sums.shape[1] - LANES
            after, total = sums[:, :n_keys], sums[:, n_keys:]
            if firsts[u]:
                w = jnp.exp2(log_betas.pop(u) + after)
                carry_ref[slots[u]] = total
            else:
                carry = carry_ref[slots[u]]
                w = jnp.exp2(log_betas.pop(u) + after + jnp.concatenate([carry] * (n_keys // LANES), axis=1))
                carry_ref[slots[u]] = carry + total
            if mask is not None:
                w = jnp.where(mask, w, 0.0)
            ws[u] = w.astype(BF16)
        u = step - 2 * ATT_SKEW
        if 0 <= u < n:
            pv = _dot(ws.pop(u), vts[u])
            acc_ref[slots[u]] = pv if firsts[u] else acc_ref[slots[u]] + pv


def _attn_prompt_kernel(q_ref, k_ref, v_ref, o_ref, carry_s, acc_s):
    i = pl.program_id(2)
    tq = q_ref.shape[1] // 2
    n_heads = q_ref.shape[2] // HEAD_DIM
    later = _later_keys(tq)
    row = lax.broadcasted_iota(jnp.int32, (tq, tq), 0)
    col = lax.broadcasted_iota(jnp.int32, (tq, tq), 1)
    diagonal = col < row
    heads = [slice(h * HEAD_DIM, (h + 1) * HEAD_DIM) for h in range(n_heads)]
    qs = {part: [q_ref[0, part * tq:(part + 1) * tq, hs] for hs in heads] for part in (0, 1)}

    def sweep(work):
        q_list, kts, vts, masks, firsts, slots = [], [], [], [], [], []
        for part, j, on_diagonal in work:
            start = pl.multiple_of(j * tq, tq)
            q_list += qs[part]
            kts += [k_ref[0, pl.ds(start, tq), hs] for hs in heads]
            vts += [v_ref[0, pl.ds(start, tq), hs] for hs in heads]
            masks += [diagonal if on_diagonal else None] * n_heads
            firsts += [on_diagonal] * n_heads
            slots += [part * n_heads + h for h in range(n_heads)]
        _sb_tiles(q_list, kts, vts, later, carry_s, acc_s, masks, firsts, slots=slots)

    sweep([(1, 2 * i + 1, True), (0, 2 * i, True), (1, 2 * i, False)])

    def pair(n, _):
        j = 2 * i - 1 - 2 * n
        sweep([(0, j, False), (1, j, False), (0, j - 1, False), (1, j - 1, False)])
        return 0

    lax.fori_loop(0, i, pair, 0)
    for part in (0, 1):
        for h, hs in enumerate(heads):
            o_ref[0, part * tq:(part + 1) * tq, hs] = acc_s[part * n_heads + h].astype(o_ref.dtype)


def _attn_prompt(qb, kb, vb):
    bsz, seq, width = qb.shape
    tq = min(ATT_TILE, seq // 2)
    lanes = ATT_HEADS * HEAD_DIM
    grid = (bsz, width // lanes, seq // (2 * tq))
    q_spec = pl.BlockSpec((1, 2 * tq, lanes), lambda b, h, i: (b, i, h))
    kv_spec = pl.BlockSpec((1, seq, lanes), lambda b, h, i: (b, 0, h))
    return pl.pallas_call(
        _attn_prompt_kernel,
        grid=grid,
        in_specs=[q_spec, kv_spec, kv_spec],
        out_specs=q_spec,
        out_shape=jax.ShapeDtypeStruct((bsz, seq, width), BF16),
        scratch_shapes=[pltpu.VMEM((2 * ATT_HEADS, tq, LANES), F32),
                        pltpu.VMEM((2 * ATT_HEADS, tq, HEAD_DIM), F32)],
        compiler_params=_params(("arbitrary", "arbitrary", "arbitrary")),
        name="attn_prompt",
    )(qb, kb, vb)


def _attn_sample_kernel(qbd_ref, kn_ref, vn_ref, kc_ref, vc_ref, o_ref, carry_s, acc_s, *, t_q):
    j = pl.program_id(1)
    n_j = pl.num_programs(1)
    rows = qbd_ref.shape[1]
    q = qbd_ref[0]

    @pl.when(j == 0)
    def _():
        n_new = kn_ref.shape[1]
        t = lax.broadcasted_iota(jnp.int32, (rows, n_new), 0) % t_q
        s = lax.broadcasted_iota(jnp.int32, (rows, n_new), 1)
        _sb_tiles([q], [kn_ref[0]], [vn_ref[0]], _later_keys(n_new), carry_s, acc_s, [s < t], [True])

    @pl.when(j > 0)
    def _():
        n_tiles = kc_ref.shape[1] // ATT_TILE
        later = _later_keys(ATT_TILE)

        def tile(n, _):
            start = pl.multiple_of((n_tiles - 1 - n) * ATT_TILE, ATT_TILE)
            kt = _merge_heads(kc_ref[0, pl.ds(start, ATT_TILE)]).astype(BF16)
            vt = _merge_heads(vc_ref[0, pl.ds(start, ATT_TILE)]).astype(BF16)
            _sb_tiles([q], [kt], [vt], later, carry_s, acc_s, [None], [False])
            return 0

        lax.fori_loop(0, n_tiles, tile, 0)

    @pl.when(j == n_j - 1)
    def _():
        for h in range(N_HEADS):
            o_ref[0, :, h * HEAD_DIM:(h + 1) * HEAD_DIM] = acc_s[
                0, h * t_q:(h + 1) * t_q, h * HEAD_DIM:(h + 1) * HEAD_DIM].astype(o_ref.dtype)


def _attn_sample(qbd, k_new, v_new, cache_k, cache_v, *, t_q, chunk):
    bsz, rows, width = qbd.shape
    past = cache_k.shape[1]
    n_chunks = past // chunk
    n_new = k_new.shape[1]
    cache_spec = pl.BlockSpec((1, chunk, N_HEADS, HEAD_DIM),
                              lambda b, j: (b, jnp.minimum(n_chunks - j, n_chunks - 1), 0, 0))
    new_spec = pl.BlockSpec((1, n_new, width), lambda b, j: (b, 0, 0))
    kern = functools.partial(_attn_sample_kernel, t_q=t_q)
    return pl.pallas_call(
        kern,
        grid=(bsz, n_chunks + 1),
        in_specs=[pl.BlockSpec((1, rows, width), lambda b, j: (b, 0, 0)), new_spec, new_spec,
                  cache_spec, cache_spec],
        out_specs=pl.BlockSpec((1, t_q, width), lambda b, j: (b, 0, 0)),
        out_shape=jax.ShapeDtypeStruct((bsz, t_q, width), BF16),
        scratch_shapes=[pltpu.VMEM((1, rows, LANES), F32), pltpu.VMEM((1, rows, width), F32)],
        compiler_params=_params(("arbitrary", "arbitrary")),
        name="attn_sample",
    )(qbd, k_new, v_new, cache_k, cache_v)


def _gate_weights(wr, wi):
    per = LRU_GROUP // wr.shape[-1]
    n_grp = wr.shape[0] // per
    eye = jnp.eye(per, dtype=wr.dtype)

    def expand(w):
        w = w.reshape(n_grp, per, w.shape[1], w.shape[2])
        return jnp.einsum('gnde,nm->gndme', w, eye).reshape(n_grp, LRU_GROUP, LRU_GROUP)

    return (0.5 * jnp.concatenate([expand(wr), expand(wi)], axis=-1)).astype(BF16)


def _block_diag_queries(qb, t_q):
    bsz = qb.shape[0]
    q = qb.reshape(bsz, t_q, N_HEADS, HEAD_DIM).transpose(0, 2, 1, 3)
    eye = jnp.eye(N_HEADS, dtype=qb.dtype)
    return jnp.einsum('bhtd,hg->bhtgd', q, eye).reshape(bsz, N_HEADS * t_q, N_HEADS * HEAD_DIM)


def _trunk(x, lru_h, lru_conv, ffn_conv, cache, w, *, n_streams, t_len, qkv_rows):
    bsz, seq, width = x.shape
    row = lambda v: v.reshape(1, -1)
    x1, h_new, c_new = _rglru(
        x, lru_h[0][:, None, :], lru_conv[0], row(w['a_norm'][0]), w['a_w_in'][0], w['a_conv_w'][0],
        row(w['a_conv_b'][0]), w['a_gate'], row(0.5 * w['a_br'][0]), row(0.5 * w['a_bi'][0]),
        row(w['a_lambda'][0]), w['a_w_out_half'][0], n_streams=n_streams, t_len=t_len)
    x2, f0_new = _ffn(x1, ffn_conv[0], row(w['f_norm'][0]), w['f_w_up'][0], w['f_conv_w'][0],
                      row(w['f_conv_b'][0]), w['f_w_down'][0], n_streams=n_streams, t_len=t_len)
    k, v, kb, vb, qb = _qkv(x2.reshape(bsz * seq, width), row(w['kv_norm']), w['w_kv'], row(w['k_norm']),
                            row(w['b_norm'][0]), w['b_wq'][0], row(w['q_norm'][0]), rows=qkv_rows)
    shape3 = (bsz, seq, width)
    kb, vb, qb = kb.reshape(shape3), vb.reshape(shape3), qb.reshape(shape3)
    if cache is None:
        o = _attn_prompt(qb, kb, vb)
    else:
        pad = ((0, 0), (0, ATT_NEW_TILE - seq), (0, 0))
        o = _attn_sample(_block_diag_queries(qb, seq), jnp.pad(kb, pad), jnp.pad(vb, pad),
                         cache[0], cache[1], t_q=seq, chunk=CACHE_CHUNK)
    y, f1_new = _ffn(x2, ffn_conv[1], row(w['f_norm'][1]), w['f_w_up'][1], w['f_conv_w'][1],
                     row(w['f_conv_b'][1]), w['f_w_down'][1], n_streams=n_streams, t_len=t_len,
                     attn=(o, w['b_wo'][0]), out_norm=row(w['out_norm']))
    heads = (bsz, seq, N_HEADS, HEAD_DIM)
    return (y, h_new.reshape(1, bsz, width), c_new[None], jnp.stack([f0_new, f1_new]),
            k.reshape(heads), v.reshape(heads))


def kernel(x_prompt, x_sample, state_lru_h, state_lru_conv, state_ffn_conv, cache_k, cache_v, a_norm, a_w_in, a_conv_w, a_conv_b, a_wr, a_br, a_wi, a_bi, a_lambda, a_w_out, kv_norm, w_kv, k_norm, b_norm, b_wq, q_norm, b_wo, f_norm, f_w_up, f_conv_w, f_conv_b, f_w_down, out_norm):
    w = dict(a_norm=a_norm, a_w_in=a_w_in.astype(BF16), a_conv_w=a_conv_w, a_conv_b=a_conv_b,
             a_gate=_gate_weights(a_wr[0], a_wi[0]), a_br=a_br, a_bi=a_bi, a_lambda=a_lambda,
             a_w_out_half=(0.5 * a_w_out).astype(BF16), kv_norm=kv_norm, w_kv=w_kv.astype(BF16), k_norm=k_norm,
             b_norm=b_norm, b_wq=b_wq.astype(BF16), q_norm=q_norm, b_wo=b_wo.astype(BF16),
             f_norm=f_norm, f_w_up=f_w_up.astype(BF16), f_conv_w=f_conv_w, f_conv_b=f_conv_b,
             f_w_down=f_w_down.astype(BF16), out_norm=out_norm)
    bp, sp, width = x_prompt.shape
    d_ff = f_w_down.shape[1]
    n_lru = a_w_in.shape[0]
    depth = f_w_up.shape[0]
    zeros = lambda *s: jnp.zeros(s, x_prompt.dtype)
    prompt = _trunk(x_prompt, zeros(n_lru, bp, width), zeros(n_lru, bp, LRU_CONV - 1, width),
                    zeros(depth, bp, FFN_CONV - 1, d_ff), None, w,
                    n_streams=1, t_len=min(ROW_BLOCK, sp), qkv_rows=min(QKV_ROWS, bp * sp))
    bs, ss, _ = x_sample.shape
    sample = _trunk(x_sample, state_lru_h, state_lru_conv, state_ffn_conv, (cache_k, cache_v), w,
                    n_streams=bs, t_len=ss, qkv_rows=bs * ss)
    y_p, h_p, c_p, f_p, k_p, v_p = prompt
    y_s, h_s, c_s, f_s, k_s, v_s = sample
    return (y_p, y_s, h_p, c_p, f_p, k_p, v_p, h_s, c_s, f_s, k_s, v_s)
```

```python
import functools
import math

import jax
import jax.numpy as jnp
from jax import lax
from jax.experimental import pallas as pl
from jax.experimental.pallas import tpu as pltpu

N_HEADS = 8
HEAD_DIM = 128
LRU_C = 8.0
LRU_CONV = 4
FFN_CONV = 3
EPS = 1e-6

SUBLANES = 8
LANES = 128
LRU_GROUP = 256
FFN_GROUPS = (256, 512, 1024, 1024, 256)
ATT_TILE = 256
ATT_Q_TILES = 4
ATT_KEY_TILES = 2
ATT_HEADS = 8
ATT_NEW_TILE = 128
ATT_SKEW = 2
ROW_BLOCK = 512
QKV_ROWS = 1024
CACHE_CHUNK = 1024
VMEM_LIMIT = 56 * 1024 * 1024

LOG2_E = math.log2(math.e)
SCORE_SCALE = HEAD_DIM ** -0.5 * LOG2_E

BF16 = jnp.bfloat16
F32 = jnp.float32


def _dot(a, b):
    return jnp.dot(a, b, preferred_element_type=F32)


def _rms(x, g):
    return x * lax.rsqrt(jnp.mean(x * x, axis=-1, keepdims=True) + EPS) * g


def _gelu(x):
    return 0.5 * x * (1.0 + jnp.tanh(0.7978845608028654 * (x + 0.044715 * (x * x * x))))


def _gelu2(x):
    return x * (1.0 + jnp.tanh(x * (0.7978845608028654 + (0.7978845608028654 * 0.044715) * (x * x))))


def _log_sigmoid(x):
    return jnp.minimum(x, 0.0) - jnp.log(1.0 + jnp.exp(-jnp.abs(x)))


def _const_spec(shape):
    nd = len(shape)
    return pl.BlockSpec(shape, lambda *_: (0,) * nd, pipeline_mode=pl.Buffered(1))


def _params(sem):
    return pltpu.CompilerParams(dimension_semantics=sem, vmem_limit_bytes=VMEM_LIMIT)


def _carried_conv(cur, tail_ref, w_ref, b_ref, lo, width, n_streams, t_len):
    w_grp = cur.shape[-1]
    outs = []
    for g in range(n_streams):
        x = cur[g * t_len:(g + 1) * t_len]
        xp = jnp.concatenate([tail_ref[g, :, lo:lo + w_grp], x], axis=0)
        y = b_ref[:, lo:lo + w_grp] + w_ref[width - 1:width, lo:lo + w_grp] * x
        for k in range(width - 1):
            shifted = pltpu.roll(xp, width - 1 - k, 0)[SUBLANES:]
            y = y + w_ref[k:k + 1, lo:lo + w_grp] * shifted
        tail_ref[g, :, lo:lo + w_grp] = x[t_len - SUBLANES:]
        outs.append(y)
    return outs[0] if n_streams == 1 else jnp.concatenate(outs, axis=0)


def _scan_by_groups(a, b, h_prev):
    t_len, width = a.shape
    n_grp = t_len // SUBLANES
    a = a.reshape(n_grp, SUBLANES, width)
    b = b.reshape(n_grp, SUBLANES, width)
    sub = lax.broadcasted_iota(jnp.int32, a.shape, 1)
    for d in (1, 2, 4):
        keep = sub >= d
        a_sh = pltpu.roll(a, d, 1)
        b_sh = pltpu.roll(b, d, 1)
        b = jnp.where(keep, a * b_sh + b, b)
        a = jnp.where(keep, a * a_sh, a)
    hs = []
    for k in range(n_grp):
        h = a[k] * h_prev + b[k]
        h_prev = h[SUBLANES - 1:SUBLANES]
        hs.append(h)
    return jnp.concatenate(hs, axis=0), h_prev


def _rglru_kernel(x_ref, h0_ref, c0_ref, norm_ref, win_ref, cw_ref, cb_ref, wg_ref, br_ref, bi_ref,
                  lam_ref, wout_ref, y_ref, hT_ref, cT_ref, h_s, tail_s,
                  *, n_streams, t_len):
    t = pl.program_id(1)
    n_t = pl.num_programs(1)
    width = x_ref.shape[-1]
    rows = n_streams * t_len

    @pl.when(t == 0)
    def _():
        h_s[...] = h0_ref[...]
        tail_s[:, SUBLANES - (LRU_CONV - 1):SUBLANES, :] = c0_ref[...]

    x = x_ref[...].reshape(rows, width)
    xn = _rms(x, norm_ref[...]).astype(BF16)

    def project(cg):
        lo = cg * LRU_GROUP
        return (_dot(xn, win_ref[:, lo:lo + LRU_GROUP]),
                _dot(xn, win_ref[:, width + lo:width + lo + LRU_GROUP]))

    def gates(cg, gate, rec):
        c = _carried_conv(rec, tail_s, cw_ref, cb_ref, cg * LRU_GROUP, LRU_CONV, n_streams, t_len)
        return gate, c, _dot(c.astype(BF16), wg_ref[cg])

    def recur(cg, gate, c, ri):
        lo = cg * LRU_GROUP
        r2 = jnp.tanh(ri[:, :LRU_GROUP] + br_ref[:, lo:lo + LRU_GROUP]) + 1.0
        i2 = jnp.tanh(ri[:, LRU_GROUP:] + bi_ref[:, lo:lo + LRU_GROUP]) + 1.0
        log_a = r2 * ((0.5 * LRU_C) * _log_sigmoid(lam_ref[:, lo:lo + LRU_GROUP]))
        a = jnp.exp(log_a)
        th = jnp.tanh(log_a)
        u = (-0.5 * th) * (1.0 / (1.0 - th))
        bt = jnp.where(u > 0.0, u * lax.rsqrt(u), 0.0) * i2 * c
        hs = []
        for g in range(n_streams):
            hg, h_last = _scan_by_groups(a[g * t_len:(g + 1) * t_len], bt[g * t_len:(g + 1) * t_len],
                                         h_s[g, :, lo:lo + LRU_GROUP])
            h_s[g, :, lo:lo + LRU_GROUP] = h_last
            hs.append(hg)
        h = hs[0] if n_streams == 1 else jnp.concatenate(hs, axis=0)
        return _dot((h * _gelu2(gate)).astype(BF16), wout_ref[lo:lo + LRU_GROUP, :])

    n_cg = width // LRU_GROUP
    projected = {cg: project(cg) for cg in range(min(2, n_cg))}
    gated = {0: gates(0, *projected.pop(0))}
    acc = jnp.zeros((rows, width), F32)
    for cg in range(n_cg):
        if cg + 2 < n_cg:
            projected[cg + 2] = project(cg + 2)
        if cg + 1 < n_cg:
            gated[cg + 1] = gates(cg + 1, *projected.pop(cg + 1))
        acc = acc + recur(cg, *gated.pop(cg))

    y_ref[...] = (x + acc).reshape(y_ref.shape)

    @pl.when(t == n_t - 1)
    def _():
        hT_ref[...] = h_s[...]
        cT_ref[...] = tail_s[:, SUBLANES - (LRU_CONV - 1):SUBLANES, :]


def _rglru(x, h0, c0, norm, w_in, conv_w, conv_b, w_gate, br, bi, lam, w_out, *, n_streams, t_len):
    bsz, seq, width = x.shape
    grid = (bsz // n_streams, seq // t_len)
    blk = lambda b, t: (b, t, 0)
    per_b = lambda b, t: (b, 0, 0)
    kern = functools.partial(_rglru_kernel, n_streams=n_streams, t_len=t_len)
    return pl.pallas_call(
        kern,
        grid=grid,
        in_specs=[
            pl.BlockSpec((n_streams, t_len, width), blk),
            pl.BlockSpec((n_streams, 1, width), per_b),
            pl.BlockSpec((n_streams, LRU_CONV - 1, width), per_b),
            _const_spec(norm.shape), _const_spec(w_in.shape), _const_spec(conv_w.shape),
            _const_spec(conv_b.shape), _const_spec(w_gate.shape), _const_spec(br.shape),
            _const_spec(bi.shape), _const_spec(lam.shape), _const_spec(w_out.shape),
        ],
        out_specs=[
            pl.BlockSpec((n_streams, t_len, width), blk),
            pl.BlockSpec((n_streams, 1, width), per_b),
            pl.BlockSpec((n_streams, LRU_CONV - 1, width), per_b),
        ],
        out_shape=[
            jax.ShapeDtypeStruct((bsz, seq, width), F32),
            jax.ShapeDtypeStruct((bsz, 1, width), F32),
            jax.ShapeDtypeStruct((bsz, LRU_CONV - 1, width), F32),
        ],
        scratch_shapes=[
            pltpu.VMEM((n_streams, 1, width), F32),
            pltpu.VMEM((n_streams, SUBLANES, width), F32),
        ],
        compiler_params=_params(("arbitrary", "arbitrary")),
        name="rglru",
    )(x, h0, c0, norm, w_in, conv_w, conv_b, w_gate, br, bi, lam, w_out)


def _ffn_kernel(*refs, n_streams, t_len, has_attn, final_norm):
    refs = list(refs)
    x_ref = refs.pop(0)
    o_ref = refs.pop(0) if has_attn else None
    wo_ref = refs.pop(0) if has_attn else None
    f0_ref, norm_ref, wup_ref, cw_ref, cb_ref, wdown_ref = refs[:6]
    refs = refs[6:]
    onorm_ref = refs.pop(0) if final_norm else None
    y_ref, fT_ref, tail_s = refs

    t = pl.program_id(1)
    n_t = pl.num_programs(1)
    width = x_ref.shape[-1]
    d_ff = wdown_ref.shape[0]
    rows = n_streams * t_len

    @pl.when(t == 0)
    def _():
        tail_s[:, SUBLANES - (FFN_CONV - 1):SUBLANES, :] = f0_ref[...]

    x = x_ref[...].reshape(rows, width)
    if has_attn:
        x = x + _dot(o_ref[...].reshape(rows, width), wo_ref[...])
    xn = _rms(x, norm_ref[...]).astype(BF16)

    bounds = [0]
    for g in FFN_GROUPS:
        bounds.append(bounds[-1] + g)

    def up(fg):
        lo, hi = bounds[fg], bounds[fg + 1]
        return (_dot(xn, wup_ref[:, lo:hi]), _dot(xn, wup_ref[:, d_ff + lo:d_ff + hi]))

    n_grp = len(FFN_GROUPS)
    acc = jnp.zeros((rows, width), F32)
    nxt = up(0)
    for fg in range(n_grp):
        lo, hi = bounds[fg], bounds[fg + 1]
        gpre, u = nxt
        if fg + 1 < n_grp:
            nxt = up(fg + 1)
        gc = _carried_conv(gpre, tail_s, cw_ref, cb_ref, lo, FFN_CONV, n_streams, t_len)
        acc = acc + _dot((_gelu(gc) * u).astype(BF16), wdown_ref[lo:hi, :])

    y = x + acc
    if final_norm:
        y = _rms(y, onorm_ref[...])
    y_ref[...] = y.reshape(y_ref.shape)

    @pl.when(t == n_t - 1)
    def _():
        fT_ref[...] = tail_s[:, SUBLANES - (FFN_CONV - 1):SUBLANES, :]


def _ffn(x, f0, norm, w_up, conv_w, conv_b, w_down, *, n_streams, t_len, attn=None, out_norm=None):
    bsz, seq, width = x.shape
    d_ff = w_down.shape[0]
    assert sum(FFN_GROUPS) == d_ff
    grid = (bsz // n_streams, seq // t_len)
    blk = lambda b, t: (b, t, 0)
    per_b = lambda b, t: (b, 0, 0)
    args = [x]
    in_specs = [pl.BlockSpec((n_streams, t_len, width), blk)]
    if attn is not None:
        o, wo = attn
        args += [o, wo]
        in_specs += [pl.BlockSpec((n_streams, t_len, width), blk), _const_spec(wo.shape)]
    args += [f0, norm, w_up, conv_w, conv_b, w_down]
    in_specs += [pl.BlockSpec((n_streams, FFN_CONV - 1, d_ff), per_b), _const_spec(norm.shape),
                 _const_spec(w_up.shape), _const_spec(conv_w.shape), _const_spec(conv_b.shape),
                 _const_spec(w_down.shape)]
    if out_norm is not None:
        args.append(out_norm)
        in_specs.append(_const_spec(out_norm.shape))
    kern = functools.partial(_ffn_kernel, n_streams=n_streams, t_len=t_len,
                             has_attn=attn is not None, final_norm=out_norm is not None)
    return pl.pallas_call(
        kern,
        grid=grid,
        in_specs=in_specs,
        out_specs=[
            pl.BlockSpec((n_streams, t_len, width), blk),
            pl.BlockSpec((n_streams, FFN_CONV - 1, d_ff), per_b),
        ],
        out_shape=[
            jax.ShapeDtypeStruct((bsz, seq, width), F32),
            jax.ShapeDtypeStruct((bsz, FFN_CONV - 1, d_ff), F32),
        ],
        scratch_shapes=[pltpu.VMEM((n_streams, SUBLANES, d_ff), F32)],
        compiler_params=_params(("arbitrary", "arbitrary")),
        name="ffn_attn" if attn is not None else "ffn",
    )(*args)


def _head_rms(x, g):
    outs = []
    for h in range(N_HEADS):
        outs.append(_rms(x[:, h * HEAD_DIM:(h + 1) * HEAD_DIM], g))
    return jnp.concatenate(outs, axis=-1)


def _split_heads(x):
    per_head = jnp.stack([x[:, h * HEAD_DIM:(h + 1) * HEAD_DIM] for h in range(N_HEADS)], axis=0)
    return jnp.transpose(per_head, (1, 0, 2))


def _merge_heads(x):
    per_head = jnp.transpose(x, (1, 0, 2))
    return jnp.concatenate([per_head[h] for h in range(N_HEADS)], axis=1)


def _qkv_kernel(x_ref, kvn_ref, wkv_ref, kn_ref, bn_ref, wq_ref, qn_ref,
                k_ref, v_ref, kb_ref, vb_ref, qb_ref):
    width = x_ref.shape[-1]
    x = x_ref[...]
    x = x * lax.rsqrt(jnp.mean(x * x, axis=-1, keepdims=True) + EPS)
    kv = _dot((x * kvn_ref[...]).astype(BF16), wkv_ref[...])
    k = _head_rms(kv[:, :width], kn_ref[...])
    v = kv[:, width:]
    k_ref[...] = _split_heads(k)
    v_ref[...] = _split_heads(v)
    kb_ref[...] = k.astype(BF16)
    vb_ref[...] = v.astype(BF16)
    q = _dot((x * bn_ref[...]).astype(BF16), wq_ref[...])
    qb_ref[...] = (_head_rms(q, qn_ref[...]) * SCORE_SCALE).astype(BF16)


def _qkv(x2d, kv_norm, w_kv, k_norm, b_norm, wq, q_norm, *, rows):
    n, width = x2d.shape
    blk = pl.BlockSpec((rows, width), lambda i: (i, 0))
    return pl.pallas_call(
        _qkv_kernel,
        grid=(n // rows,),
        in_specs=[blk, _const_spec(kv_norm.shape), _const_spec(w_kv.shape), _const_spec(k_norm.shape),
                  _const_spec(b_norm.shape), _const_spec(wq.shape), _const_spec(q_norm.shape)],
        out_specs=[pl.BlockSpec((rows, N_HEADS, HEAD_DIM), lambda i: (i, 0, 0))] * 2 + [blk] * 3,
        out_shape=[jax.ShapeDtypeStruct((n, N_HEADS, HEAD_DIM), F32)] * 2
        + [jax.ShapeDtypeStruct((n, width), BF16)] * 3,
        compiler_params=_params(("arbitrary",)),
        name="qkv",
    )(x2d, kv_norm, w_kv, k_norm, b_norm, wq, q_norm)


def _later_keys(n):
    j = lax.broadcasted_iota(jnp.int32, (n, n + LANES), 0)
    s = lax.broadcasted_iota(jnp.int32, (n, n + LANES), 1)
    return ((j > s) | (s >= n)).astype(BF16)


def _sb_tiles(qs, kts, vts, later, carry_ref, acc_ref, masks, firsts, slots=None):
    n = len(qs)
    slots = list(range(n)) if slots is None else slots
    log_betas, stays, ws = {}, {}, {}
    for step in range(n + 2 * ATT_SKEW):
        u = step
        if u < n:
            mask = masks[u]
            y = lax.dot_general(qs[u], kts[u], (((1,), (1,)), ((), ())), preferred_element_type=F32)
            log_beta = jnp.minimum(y, 0.0) - jnp.log(1.0 + jnp.exp2(-jnp.abs(y))) * LOG2_E
            log_stay = log_beta - y
            if mask is not None:
                log_stay = jnp.where(mask, log_stay, 0.0)
            log_betas[u] = log_beta
            stays[u] = log_stay.astype(BF16)
        u = step - ATT_SKEW
        if 0 <= u < n:
            mask = masks[u]
            sums = _dot(stays.pop(u), later)
            n_keys = sums.shape[1] - LANES
            after, total = sums[:, :n_keys], sums[:, n_keys:]
            if firsts[u]:
                w = jnp.exp2(log_betas.pop(u) + after)
                carry_ref[slots[u]] = total
            else:
                carry = carry_ref[slots[u]]
                w = jnp.exp2(log_betas.pop(u) + after + jnp.concatenate([carry] * (n_keys // LANES), axis=1))
                carry_ref[slots[u]] = carry + total
            if mask is not None:
                w = jnp.where(mask, w, 0.0)
            ws[u] = w.astype(BF16)
        u = step - 2 * ATT_SKEW
        if 0 <= u < n:
            pv = _dot(ws.pop(u), vts[u])
            acc_ref[slots[u]] = pv if firsts[u] else acc_ref[slots[u]] + pv


def _attn_prompt_kernel(q_ref, k_ref, v_ref, o_ref, carry_s, acc_s):
    i = pl.program_id(2)
    n_q = ATT_Q_TILES
    tq = q_ref.shape[1] // n_q
    n_heads = q_ref.shape[2] // HEAD_DIM
    later = _later_keys(tq)
    row = lax.broadcasted_iota(jnp.int32, (tq, tq), 0)
    col = lax.broadcasted_iota(jnp.int32, (tq, tq), 1)
    diagonal = col < row
    heads = [slice(h * HEAD_DIM, (h + 1) * HEAD_DIM) for h in range(n_heads)]
    qs = [[q_ref[0, part * tq:(part + 1) * tq, hs] for hs in heads] for part in range(n_q)]

    def sweep(work):
        q_list, kts, vts, masks, firsts, slots = [], [], [], [], [], []
        for part, j, on_diagonal in work:
            start = pl.multiple_of(j * tq, tq)
            q_list += qs[part]
            kts += [k_ref[0, pl.ds(start, tq), hs] for hs in heads]
            vts += [v_ref[0, pl.ds(start, tq), hs] for hs in heads]
            masks += [diagonal if on_diagonal else None] * n_heads
            firsts += [on_diagonal] * n_heads
            slots += [part * n_heads + h for h in range(n_heads)]
        _sb_tiles(q_list, kts, vts, later, carry_s, acc_s, masks, firsts, slots=slots)

    sweep([(part, n_q * i + p, part == p) for p in reversed(range(n_q)) for part in range(p, n_q)])

    def earlier(n, _):
        j = n_q * i - 1 - ATT_KEY_TILES * n
        sweep([(part, j - k, False) for k in range(ATT_KEY_TILES) for part in range(n_q)])
        return 0

    lax.fori_loop(0, n_q * i // ATT_KEY_TILES, earlier, 0)
    for part in range(n_q):
        for h, hs in enumerate(heads):
            o_ref[0, part * tq:(part + 1) * tq, hs] = acc_s[part * n_heads + h].astype(o_ref.dtype)


def _attn_prompt(qb, kb, vb):
    bsz, seq, width = qb.shape
    tq = min(ATT_TILE, seq // ATT_Q_TILES)
    lanes = ATT_HEADS * HEAD_DIM
    grid = (bsz, width // lanes, seq // (ATT_Q_TILES * tq))
    q_spec = pl.BlockSpec((1, ATT_Q_TILES * tq, lanes), lambda b, h, i: (b, i, h))
    kv_spec = pl.BlockSpec((1, seq, lanes), lambda b, h, i: (b, 0, h))
    return pl.pallas_call(
        _attn_prompt_kernel,
        grid=grid,
        in_specs=[q_spec, kv_spec, kv_spec],
        out_specs=q_spec,
        out_shape=jax.ShapeDtypeStruct((bsz, seq, width), BF16),
        scratch_shapes=[pltpu.VMEM((ATT_Q_TILES * ATT_HEADS, tq, LANES), F32),
                        pltpu.VMEM((ATT_Q_TILES * ATT_HEADS, tq, HEAD_DIM), F32)],
        compiler_params=_params(("arbitrary", "arbitrary", "arbitrary")),
        name="attn_prompt",
    )(qb, kb, vb)


def _attn_sample_kernel(qbd_ref, kn_ref, vn_ref, kc_ref, vc_ref, o_ref, carry_s, acc_s, *, t_q):
    j = pl.program_id(1)
    n_j = pl.num_programs(1)
    rows = qbd_ref.shape[1]
    q = qbd_ref[0]

    @pl.when(j == 0)
    def _():
        n_new = kn_ref.shape[1]
        t = lax.broadcasted_iota(jnp.int32, (rows, n_new), 0) % t_q
        s = lax.broadcasted_iota(jnp.int32, (rows, n_new), 1)
        _sb_tiles([q], [kn_ref[0]], [vn_ref[0]], _later_keys(n_new), carry_s, acc_s, [s < t], [True])

    @pl.when(j > 0)
    def _():
        n_tiles = kc_ref.shape[1] // ATT_TILE
        later = _later_keys(ATT_TILE)

        def tile(n, _):
            start = pl.multiple_of((n_tiles - 1 - n) * ATT_TILE, ATT_TILE)
            kt = _merge_heads(kc_ref[0, pl.ds(start, ATT_TILE)]).astype(BF16)
            vt = _merge_heads(vc_ref[0, pl.ds(start, ATT_TILE)]).astype(BF16)
            _sb_tiles([q], [kt], [vt], later, carry_s, acc_s, [None], [False])
            return 0

        lax.fori_loop(0, n_tiles, tile, 0)

    @pl.when(j == n_j - 1)
    def _():
        for h in range(N_HEADS):
            o_ref[0, :, h * HEAD_DIM:(h + 1) * HEAD_DIM] = acc_s[
                0, h * t_q:(h + 1) * t_q, h * HEAD_DIM:(h + 1) * HEAD_DIM].astype(o_ref.dtype)


def _attn_sample(qbd, k_new, v_new, cache_k, cache_v, *, t_q, chunk):
    bsz, rows, width = qbd.shape
    past = cache_k.shape[1]
    n_chunks = past // chunk
    n_new = k_new.shape[1]
    cache_spec = pl.BlockSpec((1, chunk, N_HEADS, HEAD_DIM),
                              lambda b, j: (b, jnp.minimum(n_chunks - j, n_chunks - 1), 0, 0))
    new_spec = pl.BlockSpec((1, n_new, width), lambda b, j: (b, 0, 0))
    kern = functools.partial(_attn_sample_kernel, t_q=t_q)
    return pl.pallas_call(
        kern,
        grid=(bsz, n_chunks + 1),
        in_specs=[pl.BlockSpec((1, rows, width), lambda b, j: (b, 0, 0)), new_spec, new_spec,
                  cache_spec, cache_spec],
        out_specs=pl.BlockSpec((1, t_q, width), lambda b, j: (b, 0, 0)),
        out_shape=jax.ShapeDtypeStruct((bsz, t_q, width), BF16),
        scratch_shapes=[pltpu.VMEM((1, rows, LANES), F32), pltpu.VMEM((1, rows, width), F32)],
        compiler_params=_params(("arbitrary", "arbitrary")),
        name="attn_sample",
    )(qbd, k_new, v_new, cache_k, cache_v)


def _gate_weights(wr, wi):
    per = LRU_GROUP // wr.shape[-1]
    n_grp = wr.shape[0] // per
    eye = jnp.eye(per, dtype=wr.dtype)

    def expand(w):
        w = w.reshape(n_grp, per, w.shape[1], w.shape[2])
        return jnp.einsum('gnde,nm->gndme', w, eye).reshape(n_grp, LRU_GROUP, LRU_GROUP)

    return (0.5 * jnp.concatenate([expand(wr), expand(wi)], axis=-1)).astype(BF16)


def _block_diag_queries(qb, t_q):
    bsz = qb.shape[0]
    q = qb.reshape(bsz, t_q, N_HEADS, HEAD_DIM).transpose(0, 2, 1, 3)
    eye = jnp.eye(N_HEADS, dtype=qb.dtype)
    return jnp.einsum('bhtd,hg->bhtgd', q, eye).reshape(bsz, N_HEADS * t_q, N_HEADS * HEAD_DIM)


def _trunk(x, lru_h, lru_conv, ffn_conv, cache, w, *, n_streams, t_len, qkv_rows):
    bsz, seq, width = x.shape
    row = lambda v: v.reshape(1, -1)
    x1, h_new, c_new = _rglru(
        x, lru_h[0][:, None, :], lru_conv[0], row(w['a_norm'][0]), w['a_w_in'][0], w['a_conv_w'][0],
        row(w['a_conv_b'][0]), w['a_gate'], row(0.5 * w['a_br'][0]), row(0.5 * w['a_bi'][0]),
        row(w['a_lambda'][0]), w['a_w_out_half'][0], n_streams=n_streams, t_len=t_len)
    x2, f0_new = _ffn(x1, ffn_conv[0], row(w['f_norm'][0]), w['f_w_up'][0], w['f_conv_w'][0],
                      row(w['f_conv_b'][0]), w['f_w_down'][0], n_streams=n_streams, t_len=t_len)
    k, v, kb, vb, qb = _qkv(x2.reshape(bsz * seq, width), row(w['kv_norm']), w['w_kv'], row(w['k_norm']),
                            row(w['b_norm'][0]), w['b_wq'][0], row(w['q_norm'][0]), rows=qkv_rows)
    shape3 = (bsz, seq, width)
    kb, vb, qb = kb.reshape(shape3), vb.reshape(shape3), qb.reshape(shape3)
    if cache is None:
        o = _attn_prompt(qb, kb, vb)
    else:
        pad = ((0, 0), (0, ATT_NEW_TILE - seq), (0, 0))
        o = _attn_sample(_block_diag_queries(qb, seq), jnp.pad(kb, pad), jnp.pad(vb, pad),
                         cache[0], cache[1], t_q=seq, chunk=CACHE_CHUNK)
    y, f1_new = _ffn(x2, ffn_conv[1], row(w['f_norm'][1]), w['f_w_up'][1], w['f_conv_w'][1],
                     row(w['f_conv_b'][1]), w['f_w_down'][1], n_streams=n_streams, t_len=t_len,
                     attn=(o, w['b_wo'][0]), out_norm=row(w['out_norm']))
    heads = (bsz, seq, N_HEADS, HEAD_DIM)
    return (y, h_new.reshape(1, bsz, width), c_new[None], jnp.stack([f0_new, f1_new]),
            k.reshape(heads), v.reshape(heads))


def kernel(x_prompt, x_sample, state_lru_h, state_lru_conv, state_ffn_conv, cache_k, cache_v, a_norm, a_w_in, a_conv_w, a_conv_b, a_wr, a_br, a_wi, a_bi, a_lambda, a_w_out, kv_norm, w_kv, k_norm, b_norm, b_wq, q_norm, b_wo, f_norm, f_w_up, f_conv_w, f_conv_b, f_w_down, out_norm):
    w = dict(a_norm=a_norm, a_w_in=a_w_in.astype(BF16), a_conv_w=a_conv_w, a_conv_b=a_conv_b,
             a_gate=_gate_weights(a_wr[0], a_wi[0]), a_br=a_br, a_bi=a_bi, a_lambda=a_lambda,
             a_w_out_half=(0.5 * a_w_out).astype(BF16), kv_norm=kv_norm, w_kv=w_kv.astype(BF16), k_norm=k_norm,
             b_norm=b_norm, b_wq=b_wq.astype(BF16), q_norm=q_norm, b_wo=b_wo.astype(BF16),
             f_norm=f_norm, f_w_up=f_w_up.astype(BF16), f_conv_w=f_conv_w, f_conv_b=f_conv_b,
             f_w_down=f_w_down.astype(BF16), out_norm=out_norm)
    bp, sp, width = x_prompt.shape
    d_ff = f_w_down.shape[1]
    n_lru = a_w_in.shape[0]
    depth = f_w_up.shape[0]
    zeros = lambda *s: jnp.zeros(s, x_prompt.dtype)
    prompt = _trunk(x_prompt, zeros(n_lru, bp, width), zeros(n_lru, bp, LRU_CONV - 1, width),
                    zeros(depth, bp, FFN_CONV - 1, d_ff), None, w,
                    n_streams=1, t_len=min(ROW_BLOCK, sp), qkv_rows=min(QKV_ROWS, bp * sp))
    bs, ss, _ = x_sample.shape
    sample = _trunk(x_sample, state_lru_h, state_lru_conv, state_ffn_conv, (cache_k, cache_v), w,
                    n_streams=bs, t_len=ss, qkv_rows=bs * ss)
    y_p, h_p, c_p, f_p, k_p, v_p = prompt
    y_s, h_s, c_s, f_s, k_s, v_s = sample
    return (y_p, y_s, h_p, c_p, f_p, k_p, v_p, h_s, c_s, f_s, k_s, v_s)
```

```python
import functools
import math

import jax
import jax.numpy as jnp
from jax import lax
from jax.experimental import pallas as pl
from jax.experimental.pallas import tpu as pltpu

N_HEADS = 8
HEAD_DIM = 128
LRU_C = 8.0
LRU_CONV = 4
FFN_CONV = 3
EPS = 1e-6

SUBLANES = 8
LANES = 128
LRU_GROUP = 256
FFN_GROUPS = (256, 512, 1024, 1024, 256)
ATT_TILE = 256
ATT_Q_TILES = 4
ATT_KEY_TILES = 2
ATT_HEADS = 8
ATT_NEW_TILE = 128
ATT_SKEW = 2
ROW_BLOCK = 512
QKV_ROWS = 1024
CACHE_CHUNK = 1024
VMEM_LIMIT = 56 * 1024 * 1024

LOG2_E = math.log2(math.e)
SCORE_SCALE = HEAD_DIM ** -0.5 * LOG2_E

BF16 = jnp.bfloat16
F32 = jnp.float32


def _dot(a, b):
    return jnp.dot(a, b, preferred_element_type=F32)


def _rms(x, g):
    return x * lax.rsqrt(jnp.mean(x * x, axis=-1, keepdims=True) + EPS) * g


def _gelu(x):
    return 0.5 * x * (1.0 + jnp.tanh(0.7978845608028654 * (x + 0.044715 * (x * x * x))))


def _gelu2(x):
    return x * (1.0 + jnp.tanh(x * (0.7978845608028654 + (0.7978845608028654 * 0.044715) * (x * x))))


def _log_sigmoid(x):
    return jnp.minimum(x, 0.0) - jnp.log(1.0 + jnp.exp(-jnp.abs(x)))


def _const_spec(shape):
    nd = len(shape)
    return pl.BlockSpec(shape, lambda *_: (0,) * nd, pipeline_mode=pl.Buffered(1))


def _layer_spec(shape, layer):
    nd = len(shape)
    return pl.BlockSpec((None,) + tuple(shape[1:]), lambda *_: (layer,) + (0,) * (nd - 1),
                        pipeline_mode=pl.Buffered(1))


def _params(sem):
    return pltpu.CompilerParams(dimension_semantics=sem, vmem_limit_bytes=VMEM_LIMIT)


def _carried_conv(cur, tail_ref, w_ref, b_ref, lo, width, n_streams, t_len):
    w_grp = cur.shape[-1]
    outs = []
    for g in range(n_streams):
        x = cur[g * t_len:(g + 1) * t_len]
        xp = jnp.concatenate([tail_ref[g, :, lo:lo + w_grp], x], axis=0)
        y = b_ref[:, lo:lo + w_grp] + w_ref[width - 1:width, lo:lo + w_grp] * x
        for k in range(width - 1):
            shifted = pltpu.roll(xp, width - 1 - k, 0)[SUBLANES:]
            y = y + w_ref[k:k + 1, lo:lo + w_grp] * shifted
        tail_ref[g, :, lo:lo + w_grp] = x[t_len - SUBLANES:]
        outs.append(y)
    return outs[0] if n_streams == 1 else jnp.concatenate(outs, axis=0)


def _scan_by_groups(a, b, h_prev):
    t_len, width = a.shape
    n_grp = t_len // SUBLANES
    a = a.reshape(n_grp, SUBLANES, width)
    b = b.reshape(n_grp, SUBLANES, width)
    sub = lax.broadcasted_iota(jnp.int32, a.shape, 1)
    for d in (1 << p for p in range(SUBLANES.bit_length() - 1)):
        keep = sub >= d
        a_sh = pltpu.roll(a, d, 1)
        b_sh = pltpu.roll(b, d, 1)
        b = jnp.where(keep, a * b_sh + b, b)
        a = jnp.where(keep, a * a_sh, a)
    hs = []
    for k in range(n_grp):
        h = a[k] * h_prev + b[k]
        h_prev = h[SUBLANES - 1:SUBLANES]
        hs.append(h)
    return jnp.concatenate(hs, axis=0), h_prev


def _rglru_kernel(x_ref, h0_ref, c0_ref, norm_ref, win_ref, cw_ref, cb_ref, wg_ref, br_ref, bi_ref,
                  lam_ref, wout_ref, y_ref, hT_ref, cT_ref, h_s, tail_s,
                  *, n_streams, t_len):
    t = pl.program_id(1)
    n_t = pl.num_programs(1)
    width = x_ref.shape[-1]
    rows = n_streams * t_len

    @pl.when(t == 0)
    def _():
        h_s[...] = h0_ref[...]
        tail_s[:, SUBLANES - (LRU_CONV - 1):SUBLANES, :] = c0_ref[...]

    x = x_ref[...].reshape(rows, width)
    xn = _rms(x, norm_ref[...]).astype(BF16)

    def project(cg):
        lo = cg * LRU_GROUP
        return (_dot(xn, win_ref[:, lo:lo + LRU_GROUP]),
                _dot(xn, win_ref[:, width + lo:width + lo + LRU_GROUP]))

    def gates(cg, gate, rec):
        c = _carried_conv(rec, tail_s, cw_ref, cb_ref, cg * LRU_GROUP, LRU_CONV, n_streams, t_len)
        return gate, c, _dot(c.astype(BF16), wg_ref[cg])

    def recur(cg, gate, c, ri):
        lo = cg * LRU_GROUP
        r2 = jnp.tanh(ri[:, :LRU_GROUP] + br_ref[:, lo:lo + LRU_GROUP]) + 1.0
        i2 = jnp.tanh(ri[:, LRU_GROUP:] + bi_ref[:, lo:lo + LRU_GROUP]) + 1.0
        log_a = r2 * ((0.5 * LRU_C) * _log_sigmoid(lam_ref[:, lo:lo + LRU_GROUP]))
        a = jnp.exp(log_a)
        th = jnp.tanh(log_a)
        u = (-0.5 * th) * (1.0 / (1.0 - th))
        bt = jnp.where(u > 0.0, u * lax.rsqrt(u), 0.0) * i2 * c
        hs = []
        for g in range(n_streams):
            hg, h_last = _scan_by_groups(a[g * t_len:(g + 1) * t_len], bt[g * t_len:(g + 1) * t_len],
                                         h_s[g, :, lo:lo + LRU_GROUP])
            h_s[g, :, lo:lo + LRU_GROUP] = h_last
            hs.append(hg)
        h = hs[0] if n_streams == 1 else jnp.concatenate(hs, axis=0)
        return _dot((h * _gelu2(gate)).astype(BF16), wout_ref[lo:lo + LRU_GROUP, :])

    n_cg = width // LRU_GROUP
    projected = {cg: project(cg) for cg in range(min(2, n_cg))}
    gated = {0: gates(0, *projected.pop(0))}
    acc = jnp.zeros((rows, width), F32)
    for cg in range(n_cg):
        if cg + 2 < n_cg:
            projected[cg + 2] = project(cg + 2)
        if cg + 1 < n_cg:
            gated[cg + 1] = gates(cg + 1, *projected.pop(cg + 1))
        acc = acc + recur(cg, *gated.pop(cg))

    y_ref[...] = (x + acc).reshape(y_ref.shape)

    @pl.when(t == n_t - 1)
    def _():
        hT_ref[...] = h_s[...]
        cT_ref[...] = tail_s[:, SUBLANES - (LRU_CONV - 1):SUBLANES, :]


def _rglru(x, h0, c0, norm, w_in, conv_w, conv_b, w_gate, br, bi, lam, w_out, *, n_streams, t_len):
    bsz, seq, width = x.shape
    grid = (bsz // n_streams, seq // t_len)
    blk = lambda b, t: (b, t, 0)
    per_b = lambda b, t: (b, 0, 0)
    kern = functools.partial(_rglru_kernel, n_streams=n_streams, t_len=t_len)
    return pl.pallas_call(
        kern,
        grid=grid,
        in_specs=[
            pl.BlockSpec((n_streams, t_len, width), blk),
            pl.BlockSpec((n_streams, 1, width), per_b),
            pl.BlockSpec((n_streams, LRU_CONV - 1, width), per_b),
            _const_spec(norm.shape), _const_spec(w_in.shape), _const_spec(conv_w.shape),
            _const_spec(conv_b.shape), _const_spec(w_gate.shape), _const_spec(br.shape),
            _const_spec(bi.shape), _const_spec(lam.shape), _const_spec(w_out.shape),
        ],
        out_specs=[
            pl.BlockSpec((n_streams, t_len, width), blk),
            pl.BlockSpec((n_streams, 1, width), per_b),
            pl.BlockSpec((n_streams, LRU_CONV - 1, width), per_b),
        ],
        out_shape=[
            jax.ShapeDtypeStruct((bsz, seq, width), F32),
            jax.ShapeDtypeStruct((bsz, 1, width), F32),
            jax.ShapeDtypeStruct((bsz, LRU_CONV - 1, width), F32),
        ],
        scratch_shapes=[
            pltpu.VMEM((n_streams, 1, width), F32),
            pltpu.VMEM((n_streams, SUBLANES, width), F32),
        ],
        compiler_params=_params(("arbitrary", "arbitrary")),
        name="rglru",
    )(x, h0, c0, norm, w_in, conv_w, conv_b, w_gate, br, bi, lam, w_out)


def _ffn_kernel(*refs, n_streams, t_len, has_attn, final_norm):
    refs = list(refs)
    x_ref = refs.pop(0)
    o_ref = refs.pop(0) if has_attn else None
    wo_ref = refs.pop(0) if has_attn else None
    f0_ref, norm_ref, wup_ref, cw_ref, cb_ref, wdown_ref = refs[:6]
    refs = refs[6:]
    onorm_ref = refs.pop(0) if final_norm else None
    y_ref, fT_ref, tail_s = refs

    t = pl.program_id(1)
    n_t = pl.num_programs(1)
    width = x_ref.shape[-1]
    d_ff = wdown_ref.shape[0]
    rows = n_streams * t_len

    @pl.when(t == 0)
    def _():
        tail_s[:, SUBLANES - (FFN_CONV - 1):SUBLANES, :] = f0_ref[...]

    x = x_ref[...].reshape(rows, width)
    if has_attn:
        x = x + _dot(o_ref[...].reshape(rows, width), wo_ref[...])
    xn = _rms(x, norm_ref[...]).astype(BF16)

    bounds = [0]
    for g in FFN_GROUPS:
        bounds.append(bounds[-1] + g)

    def up(fg):
        lo, hi = bounds[fg], bounds[fg + 1]
        return (_dot(xn, wup_ref[:, lo:hi]), _dot(xn, wup_ref[:, d_ff + lo:d_ff + hi]))

    n_grp = len(FFN_GROUPS)
    acc = jnp.zeros((rows, width), F32)
    nxt = up(0)
    for fg in range(n_grp):
        lo, hi = bounds[fg], bounds[fg + 1]
        gpre, u = nxt
        if fg + 1 < n_grp:
            nxt = up(fg + 1)
        gc = _carried_conv(gpre, tail_s, cw_ref, cb_ref, lo, FFN_CONV, n_streams, t_len)
        acc = acc + _dot((_gelu(gc) * u).astype(BF16), wdown_ref[lo:hi, :])

    y = x + acc
    if final_norm:
        y = _rms(y, onorm_ref[...])
    y_ref[...] = y.reshape(y_ref.shape)

    @pl.when(t == n_t - 1)
    def _():
        fT_ref[...] = tail_s[:, SUBLANES - (FFN_CONV - 1):SUBLANES, :]


def _ffn(x, f0, norm, w_up, conv_w, conv_b, w_down, *, layer, n_streams, t_len, attn=None, out_norm=None):
    bsz, seq, width = x.shape
    d_ff = w_down.shape[1]
    assert sum(FFN_GROUPS) == d_ff
    grid = (bsz // n_streams, seq // t_len)
    blk = lambda b, t: (b, t, 0)
    per_b = lambda b, t: (b, 0, 0)
    args = [x]
    in_specs = [pl.BlockSpec((n_streams, t_len, width), blk)]
    if attn is not None:
        o, wo = attn
        args += [o, wo]
        in_specs += [pl.BlockSpec((n_streams, t_len, width), blk), _const_spec(wo.shape)]
    args += [f0, norm, w_up, conv_w, conv_b, w_down]
    in_specs += [pl.BlockSpec((n_streams, FFN_CONV - 1, d_ff), per_b), _const_spec(norm.shape),
                 _layer_spec(w_up.shape, layer), _const_spec(conv_w.shape), _const_spec(conv_b.shape),
                 _layer_spec(w_down.shape, layer)]
    if out_norm is not None:
        args.append(out_norm)
        in_specs.append(_const_spec(out_norm.shape))
    kern = functools.partial(_ffn_kernel, n_streams=n_streams, t_len=t_len,
                             has_attn=attn is not None, final_norm=out_norm is not None)
    return pl.pallas_call(
        kern,
        grid=grid,
        in_specs=in_specs,
        out_specs=[
            pl.BlockSpec((n_streams, t_len, width), blk),
            pl.BlockSpec((n_streams, FFN_CONV - 1, d_ff), per_b),
        ],
        out_shape=[
            jax.ShapeDtypeStruct((bsz, seq, width), F32),
            jax.ShapeDtypeStruct((bsz, FFN_CONV - 1, d_ff), F32),
        ],
        scratch_shapes=[pltpu.VMEM((n_streams, SUBLANES, d_ff), F32)],
        compiler_params=_params(("arbitrary", "arbitrary")),
        name="ffn_attn" if attn is not None else "ffn",
    )(*args)


def _head_rms(x, g):
    outs = []
    for h in range(N_HEADS):
        outs.append(_rms(x[:, h * HEAD_DIM:(h + 1) * HEAD_DIM], g))
    return jnp.concatenate(outs, axis=-1)


def _split_heads(x):
    per_head = jnp.stack([x[:, h * HEAD_DIM:(h + 1) * HEAD_DIM] for h in range(N_HEADS)], axis=0)
    return jnp.transpose(per_head, (1, 0, 2))


def _merge_heads(x):
    per_head = jnp.transpose(x, (1, 0, 2))
    return jnp.concatenate([per_head[h] for h in range(N_HEADS)], axis=1)


def _qkv_kernel(x_ref, kvn_ref, wkv_ref, kn_ref, bn_ref, wq_ref, qn_ref,
                k_ref, v_ref, kb_ref, vb_ref, qb_ref):
    width = x_ref.shape[-1]
    x = x_ref[...]
    x = x * lax.rsqrt(jnp.mean(x * x, axis=-1, keepdims=True) + EPS)
    kv = _dot((x * kvn_ref[...]).astype(BF16), wkv_ref[...])
    k = _head_rms(kv[:, :width], kn_ref[...])
    v = kv[:, width:]
    k_ref[...] = _split_heads(k)
    v_ref[...] = _split_heads(v)
    kb_ref[...] = k.astype(BF16)
    vb_ref[...] = v.astype(BF16)
    q = _dot((x * bn_ref[...]).astype(BF16), wq_ref[...])
    qb_ref[...] = (_head_rms(q, qn_ref[...]) * SCORE_SCALE).astype(BF16)


def _qkv(x2d, kv_norm, w_kv, k_norm, b_norm, wq, q_norm, *, rows):
    n, width = x2d.shape
    blk = pl.BlockSpec((rows, width), lambda i: (i, 0))
    return pl.pallas_call(
        _qkv_kernel,
        grid=(n // rows,),
        in_specs=[blk, _const_spec(kv_norm.shape), _const_spec(w_kv.shape), _const_spec(k_norm.shape),
                  _const_spec(b_norm.shape), _const_spec(wq.shape), _const_spec(q_norm.shape)],
        out_specs=[pl.BlockSpec((rows, N_HEADS, HEAD_DIM), lambda i: (i, 0, 0))] * 2 + [blk] * 3,
        out_shape=[jax.ShapeDtypeStruct((n, N_HEADS, HEAD_DIM), F32)] * 2
        + [jax.ShapeDtypeStruct((n, width), BF16)] * 3,
        compiler_params=_params(("arbitrary",)),
        name="qkv",
    )(x2d, kv_norm, w_kv, k_norm, b_norm, wq, q_norm)


def _later_keys(n):
    j = lax.broadcasted_iota(jnp.int32, (n, n + LANES), 0)
    s = lax.broadcasted_iota(jnp.int32, (n, n + LANES), 1)
    return ((j > s) | (s >= n)).astype(BF16)


def _sb_tiles(qs, kts, vts, later, carry_ref, acc_ref, masks, firsts, slots=None):
    n = len(qs)
    slots = list(range(n)) if slots is None else slots
    log_betas, stays, ws = {}, {}, {}
    for step in range(n + 2 * ATT_SKEW):
        u = step
        if u < n:
            mask = masks[u]
            y = lax.dot_general(qs[u], kts[u], (((1,), (1,)), ((), ())), preferred_element_type=F32)
            log_beta = jnp.minimum(y, 0.0) - jnp.log(1.0 + jnp.exp2(-jnp.abs(y))) * LOG2_E
            log_stay = log_beta - y
            if mask is not None:
                log_stay = jnp.where(mask, log_stay, 0.0)
            log_betas[u] = log_beta
            stays[u] = log_stay.astype(BF16)
        u = step - ATT_SKEW
        if 0 <= u < n:
            mask = masks[u]
            sums = _dot(stays.pop(u), later)
            n_keys = sums.shape[1] - LANES
            after, total = sums[:, :n_keys], sums[:, n_keys:]
            if firsts[u]:
                w = jnp.exp2(log_betas.pop(u) + after)
                carry_ref[slots[u]] = total
            else:
                carry = carry_ref[slots[u]]
                w = jnp.exp2(log_betas.pop(u) + after + jnp.concatenate([carry] * (n_keys // LANES), axis=1))
                carry_ref[slots[u]] = carry + total
            if mask is not None:
                w = jnp.where(mask, w, 0.0)
            ws[u] = w.astype(BF16)
        u = step - 2 * ATT_SKEW
        if 0 <= u < n:
            pv = _dot(ws.pop(u), vts[u])
            acc_ref[slots[u]] = pv if firsts[u] else acc_ref[slots[u]] + pv


def _attn_prompt_kernel(q_ref, k_ref, v_ref, o_ref, carry_s, acc_s):
    i = pl.program_id(2)
    n_q = ATT_Q_TILES
    tq = q_ref.shape[1] // n_q
    n_heads = q_ref.shape[2] // HEAD_DIM
    later = _later_keys(tq)
    row = lax.broadcasted_iota(jnp.int32, (tq, tq), 0)
    col = lax.broadcasted_iota(jnp.int32, (tq, tq), 1)
    diagonal = col < row
    heads = [slice(h * HEAD_DIM, (h + 1) * HEAD_DIM) for h in range(n_heads)]
    qs = [[q_ref[0, part * tq:(part + 1) * tq, hs] for hs in heads] for part in range(n_q)]

    def sweep(work):
        q_list, kts, vts, masks, firsts, slots = [], [], [], [], [], []
        for part, j, on_diagonal in work:
            start = pl.multiple_of(j * tq, tq)
            q_list += qs[part]
            kts += [k_ref[0, pl.ds(start, tq), hs] for hs in heads]
            vts += [v_ref[0, pl.ds(start, tq), hs] for hs in heads]
            masks += [diagonal if on_diagonal else None] * n_heads
            firsts += [on_diagonal] * n_heads
            slots += [part * n_heads + h for h in range(n_heads)]
        _sb_tiles(q_list, kts, vts, later, carry_s, acc_s, masks, firsts, slots=slots)

    sweep([(part, n_q * i + p, part == p) for p in reversed(range(n_q)) for part in range(p, n_q)])

    def earlier(n, _):
        j = n_q * i - 1 - ATT_KEY_TILES * n
        sweep([(part, j - k, False) for k in range(ATT_KEY_TILES) for part in range(n_q)])
        return 0

    lax.fori_loop(0, n_q * i // ATT_KEY_TILES, earlier, 0)
    for part in range(n_q):
        for h, hs in enumerate(heads):
            o_ref[0, part * tq:(part + 1) * tq, hs] = acc_s[part * n_heads + h].astype(o_ref.dtype)


def _attn_prompt(qb, kb, vb):
    bsz, seq, width = qb.shape
    tq = min(ATT_TILE, seq // ATT_Q_TILES)
    lanes = ATT_HEADS * HEAD_DIM
    grid = (bsz, width // lanes, seq // (ATT_Q_TILES * tq))
    q_spec = pl.BlockSpec((1, ATT_Q_TILES * tq, lanes), lambda b, h, i: (b, i, h))
    kv_spec = pl.BlockSpec((1, seq, lanes), lambda b, h, i: (b, 0, h))
    return pl.pallas_call(
        _attn_prompt_kernel,
        grid=grid,
        in_specs=[q_spec, kv_spec, kv_spec],
        out_specs=q_spec,
        out_shape=jax.ShapeDtypeStruct((bsz, seq, width), BF16),
        scratch_shapes=[pltpu.VMEM((ATT_Q_TILES * ATT_HEADS, tq, LANES), F32),
                        pltpu.VMEM((ATT_Q_TILES * ATT_HEADS, tq, HEAD_DIM), F32)],
        compiler_params=_params(("arbitrary", "arbitrary", "arbitrary")),
        name="attn_prompt",
    )(qb, kb, vb)


def _attn_sample_kernel(qbd_ref, kn_ref, vn_ref, kc_ref, vc_ref, o_ref, carry_s, acc_s, *, t_q):
    j = pl.program_id(1)
    n_j = pl.num_programs(1)
    rows = qbd_ref.shape[1]
    q = qbd_ref[0]

    @pl.when(j == 0)
    def _():
        n_new = kn_ref.shape[1]
        t = lax.broadcasted_iota(jnp.int32, (rows, n_new), 0) % t_q
        s = lax.broadcasted_iota(jnp.int32, (rows, n_new), 1)
        _sb_tiles([q], [kn_ref[0]], [vn_ref[0]], _later_keys(n_new), carry_s, acc_s, [s < t], [True])

    @pl.when(j > 0)
    def _():
        n_tiles = kc_ref.shape[1] // ATT_TILE
        later = _later_keys(ATT_TILE)

        def tile(n, _):
            start = pl.multiple_of((n_tiles - 1 - n) * ATT_TILE, ATT_TILE)
            kt = _merge_heads(kc_ref[0, pl.ds(start, ATT_TILE)]).astype(BF16)
            vt = _merge_heads(vc_ref[0, pl.ds(start, ATT_TILE)]).astype(BF16)
            _sb_tiles([q], [kt], [vt], later, carry_s, acc_s, [None], [False])
            return 0

        lax.fori_loop(0, n_tiles, tile, 0)

    @pl.when(j == n_j - 1)
    def _():
        for h in range(N_HEADS):
            o_ref[0, :, h * HEAD_DIM:(h + 1) * HEAD_DIM] = acc_s[
                0, h * t_q:(h + 1) * t_q, h * HEAD_DIM:(h + 1) * HEAD_DIM].astype(o_ref.dtype)


def _attn_sample(qbd, k_new, v_new, cache_k, cache_v, *, t_q, chunk):
    bsz, rows, width = qbd.shape
    past = cache_k.shape[1]
    n_chunks = past // chunk
    n_new = k_new.shape[1]
    cache_spec = pl.BlockSpec((1, chunk, N_HEADS, HEAD_DIM),
                              lambda b, j: (b, jnp.minimum(n_chunks - j, n_chunks - 1), 0, 0))
    new_spec = pl.BlockSpec((1, n_new, width), lambda b, j: (b, 0, 0))
    kern = functools.partial(_attn_sample_kernel, t_q=t_q)
    return pl.pallas_call(
        kern,
        grid=(bsz, n_chunks + 1),
        in_specs=[pl.BlockSpec((1, rows, width), lambda b, j: (b, 0, 0)), new_spec, new_spec,
                  cache_spec, cache_spec],
        out_specs=pl.BlockSpec((1, t_q, width), lambda b, j: (b, 0, 0)),
        out_shape=jax.ShapeDtypeStruct((bsz, t_q, width), BF16),
        scratch_shapes=[pltpu.VMEM((1, rows, LANES), F32), pltpu.VMEM((1, rows, width), F32)],
        compiler_params=_params(("arbitrary", "arbitrary")),
        name="attn_sample",
    )(qbd, k_new, v_new, cache_k, cache_v)


def _gate_weights(wr, wi):
    per = LRU_GROUP // wr.shape[-1]
    n_grp = wr.shape[0] // per
    eye = jnp.eye(per, dtype=wr.dtype)

    def expand(w):
        w = w.reshape(n_grp, per, w.shape[1], w.shape[2])
        return jnp.einsum('gnde,nm->gndme', w, eye).reshape(n_grp, LRU_GROUP, LRU_GROUP)

    return (0.5 * jnp.concatenate([expand(wr), expand(wi)], axis=-1)).astype(BF16)


def _block_diag_queries(qb, t_q):
    bsz = qb.shape[0]
    q = qb.reshape(bsz, t_q, N_HEADS, HEAD_DIM).transpose(0, 2, 1, 3)
    eye = jnp.eye(N_HEADS, dtype=qb.dtype)
    return jnp.einsum('bhtd,hg->bhtgd', q, eye).reshape(bsz, N_HEADS * t_q, N_HEADS * HEAD_DIM)


def _trunk(x, lru_h, lru_conv, ffn_conv, cache, w, *, n_streams, t_len, qkv_rows):
    bsz, seq, width = x.shape
    row = lambda v: v.reshape(1, -1)
    x1, h_new, c_new = _rglru(
        x, lru_h[0][:, None, :], lru_conv[0], row(w['a_norm'][0]), w['a_w_in'][0], w['a_conv_w'][0],
        row(w['a_conv_b'][0]), w['a_gate'], row(0.5 * w['a_br'][0]), row(0.5 * w['a_bi'][0]),
        row(w['a_lambda'][0]), w['a_w_out_half'][0], n_streams=n_streams, t_len=t_len)
    x2, f0_new = _ffn(x1, ffn_conv[0], row(w['f_norm'][0]), w['f_w_up'], w['f_conv_w'][0],
                      row(w['f_conv_b'][0]), w['f_w_down'], layer=0, n_streams=n_streams, t_len=t_len)
    k, v, kb, vb, qb = _qkv(x2.reshape(bsz * seq, width), row(w['kv_norm']), w['w_kv'], row(w['k_norm']),
                            row(w['b_norm'][0]), w['b_wq'][0], row(w['q_norm'][0]), rows=qkv_rows)
    shape3 = (bsz, seq, width)
    kb, vb, qb = kb.reshape(shape3), vb.reshape(shape3), qb.reshape(shape3)
    if cache is None:
        o = _attn_prompt(qb, kb, vb)
    else:
        pad = ((0, 0), (0, ATT_NEW_TILE - seq), (0, 0))
        o = _attn_sample(_block_diag_queries(qb, seq), jnp.pad(kb, pad), jnp.pad(vb, pad),
                         cache[0], cache[1], t_q=seq, chunk=CACHE_CHUNK)
    y, f1_new = _ffn(x2, ffn_conv[1], row(w['f_norm'][1]), w['f_w_up'], w['f_conv_w'][1],
                     row(w['f_conv_b'][1]), w['f_w_down'], layer=1, n_streams=n_streams, t_len=t_len,
                     attn=(o, w['b_wo'][0]), out_norm=row(w['out_norm']))
    heads = (bsz, seq, N_HEADS, HEAD_DIM)
    return (y, h_new.reshape(1, bsz, width), c_new[None], jnp.stack([f0_new, f1_new]),
            k.reshape(heads), v.reshape(heads))


def kernel(x_prompt, x_sample, state_lru_h, state_lru_conv, state_ffn_conv, cache_k, cache_v, a_norm, a_w_in, a_conv_w, a_conv_b, a_wr, a_br, a_wi, a_bi, a_lambda, a_w_out, kv_norm, w_kv, k_norm, b_norm, b_wq, q_norm, b_wo, f_norm, f_w_up, f_conv_w, f_conv_b, f_w_down, out_norm):
    w = dict(a_norm=a_norm, a_w_in=a_w_in.astype(BF16), a_conv_w=a_conv_w, a_conv_b=a_conv_b,
             a_gate=_gate_weights(a_wr[0], a_wi[0]), a_br=a_br, a_bi=a_bi, a_lambda=a_lambda,
             a_w_out_half=(0.5 * a_w_out).astype(BF16), kv_norm=kv_norm, w_kv=w_kv.astype(BF16), k_norm=k_norm,
             b_norm=b_norm, b_wq=b_wq.astype(BF16), q_norm=q_norm, b_wo=b_wo.astype(BF16),
             f_norm=f_norm, f_w_up=f_w_up.astype(BF16), f_conv_w=f_conv_w, f_conv_b=f_conv_b,
             f_w_down=f_w_down.astype(BF16), out_norm=out_norm)
    bp, sp, width = x_prompt.shape
    d_ff = f_w_down.shape[1]
    n_lru = a_w_in.shape[0]
    depth = f_w_up.shape[0]
    zeros = lambda *s: jnp.zeros(s, x_prompt.dtype)
    prompt = _trunk(x_prompt, zeros(n_lru, bp, width), zeros(n_lru, bp, LRU_CONV - 1, width),
                    zeros(depth, bp, FFN_CONV - 1, d_ff), None, w,
                    n_streams=1, t_len=min(ROW_BLOCK, sp), qkv_rows=min(QKV_ROWS, bp * sp))
    bs, ss, _ = x_sample.shape
    sample = _trunk(x_sample, state_lru_h, state_lru_conv, state_ffn_conv, (cache_k, cache_v), w,
                    n_streams=bs, t_len=ss, qkv_rows=bs * ss)
    y_p, h_p, c_p, f_p, k_p, v_p = prompt
    y_s, h_s, c_s, f_s, k_s, v_s = sample
    return (y_p, y_s, h_p, c_p, f_p, k_p, v_p, h_s, c_s, f_s, k_s, v_s)
```

```python
import functools
import math

import jax
import jax.numpy as jnp
from jax import lax
from jax.experimental import pallas as pl
from jax.experimental.pallas import tpu as pltpu

N_HEADS = 8
HEAD_DIM = 128
LRU_C = 8.0
LRU_CONV = 4
FFN_CONV = 3
EPS = 1e-6

SUBLANES = 8
LANES = 128
LRU_GROUP = 256
FFN_GROUPS = (256, 512, 1024, 1024, 256)
ATT_TILE = 256
ATT_Q_TILES = 4
ATT_KEY_TILES = 2
ATT_HEADS = 8
ATT_NEW_TILE = 128
ATT_SKEW = 2
ROW_BLOCK = 512
QKV_ROWS = 1024
CACHE_CHUNK = 1024
VMEM_LIMIT = 56 * 1024 * 1024

LOG2_E = math.log2(math.e)
SCORE_SCALE = HEAD_DIM ** -0.5 * LOG2_E

BF16 = jnp.bfloat16
F32 = jnp.float32


def _dot(a, b):
    return jnp.dot(a, b, preferred_element_type=F32)


def _rms(x, g):
    return x * lax.rsqrt(jnp.mean(x * x, axis=-1, keepdims=True) + EPS) * g


def _gelu(x):
    return 0.5 * x * (1.0 + jnp.tanh(0.7978845608028654 * (x + 0.044715 * (x * x * x))))


def _gelu2(x):
    return x * (1.0 + jnp.tanh(x * (0.7978845608028654 + (0.7978845608028654 * 0.044715) * (x * x))))


def _log_sigmoid(x):
    return jnp.minimum(x, 0.0) - jnp.log(1.0 + jnp.exp(-jnp.abs(x)))


def _const_spec(shape):
    nd = len(shape)
    return pl.BlockSpec(shape, lambda *_: (0,) * nd, pipeline_mode=pl.Buffered(1))


def _layer_spec(shape, layer):
    nd = len(shape)
    return pl.BlockSpec((None,) + tuple(shape[1:]), lambda *_: (layer,) + (0,) * (nd - 1),
                        pipeline_mode=pl.Buffered(1))


def _params(sem):
    return pltpu.CompilerParams(dimension_semantics=sem, vmem_limit_bytes=VMEM_LIMIT)


def _carried_conv(cur, tail_ref, w_ref, b_ref, lo, width, n_streams, t_len):
    w_grp = cur.shape[-1]
    outs = []
    for g in range(n_streams):
        x = cur[g * t_len:(g + 1) * t_len]
        xp = jnp.concatenate([tail_ref[g, :, lo:lo + w_grp], x], axis=0)
        y = b_ref[:, lo:lo + w_grp] + w_ref[width - 1:width, lo:lo + w_grp] * x
        for k in range(width - 1):
            shifted = pltpu.roll(xp, width - 1 - k, 0)[SUBLANES:]
            y = y + w_ref[k:k + 1, lo:lo + w_grp] * shifted
        tail_ref[g, :, lo:lo + w_grp] = x[t_len - SUBLANES:]
        outs.append(y)
    return outs[0] if n_streams == 1 else jnp.concatenate(outs, axis=0)


def _scan_by_groups(a, b, h_prev):
    t_len, width = a.shape
    n_grp = t_len // SUBLANES
    a = a.reshape(n_grp, SUBLANES, width)
    b = b.reshape(n_grp, SUBLANES, width)
    sub = lax.broadcasted_iota(jnp.int32, a.shape, 1)
    for d in (1 << p for p in range(SUBLANES.bit_length() - 1)):
        keep = sub >= d
        a_sh = pltpu.roll(a, d, 1)
        b_sh = pltpu.roll(b, d, 1)
        b = jnp.where(keep, a * b_sh + b, b)
        a = jnp.where(keep, a * a_sh, a)
    hs = []
    for k in range(n_grp):
        h = a[k] * h_prev + b[k]
        h_prev = h[SUBLANES - 1:SUBLANES]
        hs.append(h)
    return jnp.concatenate(hs, axis=0), h_prev


def _rglru_kernel(x_ref, h0_ref, c0_ref, norm_ref, win_ref, cw_ref, cb_ref, wg_ref, br_ref, bi_ref,
                  lam_ref, wout_ref, y_ref, hT_ref, cT_ref, h_s, tail_s,
                  *, n_streams, t_len):
    t = pl.program_id(1)
    n_t = pl.num_programs(1)
    width = x_ref.shape[-1]
    rows = n_streams * t_len

    @pl.when(t == 0)
    def _():
        h_s[...] = h0_ref[...]
        tail_s[:, SUBLANES - (LRU_CONV - 1):SUBLANES, :] = c0_ref[...]

    x = x_ref[...].reshape(rows, width)
    xn = _rms(x, norm_ref[...]).astype(BF16)

    def project(cg):
        lo = cg * LRU_GROUP
        return (_dot(xn, win_ref[:, lo:lo + LRU_GROUP]),
                _dot(xn, win_ref[:, width + lo:width + lo + LRU_GROUP]))

    def gates(cg, gate, rec):
        c = _carried_conv(rec, tail_s, cw_ref, cb_ref, cg * LRU_GROUP, LRU_CONV, n_streams, t_len)
        return gate, c, _dot(c.astype(BF16), wg_ref[cg])

    def recur(cg, gate, c, ri):
        lo = cg * LRU_GROUP
        r2 = jnp.tanh(ri[:, :LRU_GROUP] + br_ref[:, lo:lo + LRU_GROUP]) + 1.0
        i2 = jnp.tanh(ri[:, LRU_GROUP:] + bi_ref[:, lo:lo + LRU_GROUP]) + 1.0
        log_a = r2 * ((0.5 * LRU_C) * _log_sigmoid(lam_ref[:, lo:lo + LRU_GROUP]))
        a = jnp.exp(log_a)
        th = jnp.tanh(log_a)
        u = (-0.5 * th) * (1.0 / (1.0 - th))
        bt = jnp.where(u > 0.0, u * lax.rsqrt(u), 0.0) * i2 * c
        hs = []
        for g in range(n_streams):
            hg, h_last = _scan_by_groups(a[g * t_len:(g + 1) * t_len], bt[g * t_len:(g + 1) * t_len],
                                         h_s[g, :, lo:lo + LRU_GROUP])
            h_s[g, :, lo:lo + LRU_GROUP] = h_last
            hs.append(hg)
        h = hs[0] if n_streams == 1 else jnp.concatenate(hs, axis=0)
        return _dot((h * _gelu2(gate)).astype(BF16), wout_ref[lo:lo + LRU_GROUP, :])

    n_cg = width // LRU_GROUP
    projected = {cg: project(cg) for cg in range(min(2, n_cg))}
    gated = {0: gates(0, *projected.pop(0))}
    acc = jnp.zeros((rows, width), F32)
    for cg in range(n_cg):
        if cg + 2 < n_cg:
            projected[cg + 2] = project(cg + 2)
        if cg + 1 < n_cg:
            gated[cg + 1] = gates(cg + 1, *projected.pop(cg + 1))
        acc = acc + recur(cg, *gated.pop(cg))

    y_ref[...] = (x + acc).reshape(y_ref.shape)

    @pl.when(t == n_t - 1)
    def _():
        hT_ref[...] = h_s[...]
        cT_ref[...] = tail_s[:, SUBLANES - (LRU_CONV - 1):SUBLANES, :]


def _rglru(x, h0, c0, norm, w_in, conv_w, conv_b, w_gate, br, bi, lam, w_out, *, n_streams, t_len):
    bsz, seq, width = x.shape
    grid = (bsz // n_streams, seq // t_len)
    blk = lambda b, t: (b, t, 0)
    per_b = lambda b, t: (b, 0, 0)
    kern = functools.partial(_rglru_kernel, n_streams=n_streams, t_len=t_len)
    return pl.pallas_call(
        kern,
        grid=grid,
        in_specs=[
            pl.BlockSpec((n_streams, t_len, width), blk),
            pl.BlockSpec((n_streams, 1, width), per_b),
            pl.BlockSpec((n_streams, LRU_CONV - 1, width), per_b),
            _const_spec(norm.shape), _const_spec(w_in.shape), _const_spec(conv_w.shape),
            _const_spec(conv_b.shape), _const_spec(w_gate.shape), _const_spec(br.shape),
            _const_spec(bi.shape), _const_spec(lam.shape), _const_spec(w_out.shape),
        ],
        out_specs=[
            pl.BlockSpec((n_streams, t_len, width), blk),
            pl.BlockSpec((n_streams, 1, width), per_b),
            pl.BlockSpec((n_streams, LRU_CONV - 1, width), per_b),
        ],
        out_shape=[
            jax.ShapeDtypeStruct((bsz, seq, width), F32),
            jax.ShapeDtypeStruct((bsz, 1, width), F32),
            jax.ShapeDtypeStruct((bsz, LRU_CONV - 1, width), F32),
        ],
        scratch_shapes=[
            pltpu.VMEM((n_streams, 1, width), F32),
            pltpu.VMEM((n_streams, SUBLANES, width), F32),
        ],
        compiler_params=_params(("arbitrary", "arbitrary")),
        name="rglru",
    )(x, h0, c0, norm, w_in, conv_w, conv_b, w_gate, br, bi, lam, w_out)


def _ffn_kernel(*refs, n_streams, t_len, has_attn, final_norm):
    refs = list(refs)
    x_ref = refs.pop(0)
    o_ref = refs.pop(0) if has_attn else None
    wo_ref = refs.pop(0) if has_attn else None
    f0_ref, norm_ref, wup_ref, cw_ref, cb_ref, wdown_ref = refs[:6]
    refs = refs[6:]
    onorm_ref = refs.pop(0) if final_norm else None
    y_ref, fT_ref, tail_s = refs

    t = pl.program_id(1)
    n_t = pl.num_programs(1)
    width = x_ref.shape[-1]
    d_ff = wdown_ref.shape[0]
    rows = n_streams * t_len

    @pl.when(t == 0)
    def _():
        tail_s[:, SUBLANES - (FFN_CONV - 1):SUBLANES, :] = f0_ref[...]

    x = x_ref[...].reshape(rows, width)
    if has_attn:
        x = x + _dot(o_ref[...].reshape(rows, width), wo_ref[...])
    xn = _rms(x, norm_ref[...]).astype(BF16)

    bounds = [0]
    for g in FFN_GROUPS:
        bounds.append(bounds[-1] + g)

    def up(fg):
        lo, hi = bounds[fg], bounds[fg + 1]
        return (_dot(xn, wup_ref[:, lo:hi]), _dot(xn, wup_ref[:, d_ff + lo:d_ff + hi]))

    n_grp = len(FFN_GROUPS)
    acc = jnp.zeros((rows, width), F32)
    nxt = up(0)
    for fg in range(n_grp):
        lo, hi = bounds[fg], bounds[fg + 1]
        gpre, u = nxt
        if fg + 1 < n_grp:
            nxt = up(fg + 1)
        gc = _carried_conv(gpre, tail_s, cw_ref, cb_ref, lo, FFN_CONV, n_streams, t_len)
        acc = acc + _dot((_gelu(gc) * u).astype(BF16), wdown_ref[lo:hi, :])

    y = x + acc
    if final_norm:
        y = _rms(y, onorm_ref[...])
    y_ref[...] = y.reshape(y_ref.shape)

    @pl.when(t == n_t - 1)
    def _():
        fT_ref[...] = tail_s[:, SUBLANES - (FFN_CONV - 1):SUBLANES, :]


def _ffn(x, f0, norm, w_up, conv_w, conv_b, w_down, *, layer, n_streams, t_len, attn=None, out_norm=None):
    bsz, seq, width = x.shape
    d_ff = w_down.shape[1]
    assert sum(FFN_GROUPS) == d_ff
    grid = (bsz // n_streams, seq // t_len)
    blk = lambda b, t: (b, t, 0)
    per_b = lambda b, t: (b, 0, 0)
    args = [x]
    in_specs = [pl.BlockSpec((n_streams, t_len, width), blk)]
    if attn is not None:
        o, wo = attn
        args += [o, wo]
        in_specs += [pl.BlockSpec((n_streams, t_len, width), blk), _const_spec(wo.shape)]
    args += [f0, norm, w_up, conv_w, conv_b, w_down]
    in_specs += [pl.BlockSpec((n_streams, FFN_CONV - 1, d_ff), per_b), _const_spec(norm.shape),
                 _layer_spec(w_up.shape, layer), _const_spec(conv_w.shape), _const_spec(conv_b.shape),
                 _layer_spec(w_down.shape, layer)]
    if out_norm is not None:
        args.append(out_norm)
        in_specs.append(_const_spec(out_norm.shape))
    kern = functools.partial(_ffn_kernel, n_streams=n_streams, t_len=t_len,
                             has_attn=attn is not None, final_norm=out_norm is not None)
    return pl.pallas_call(
        kern,
        grid=grid,
        in_specs=in_specs,
        out_specs=[
            pl.BlockSpec((n_streams, t_len, width), blk),
            pl.BlockSpec((n_streams, FFN_CONV - 1, d_ff), per_b),
        ],
        out_shape=[
            jax.ShapeDtypeStruct((bsz, seq, width), F32),
            jax.ShapeDtypeStruct((bsz, FFN_CONV - 1, d_ff), F32),
        ],
        scratch_shapes=[pltpu.VMEM((n_streams, SUBLANES, d_ff), F32)],
        compiler_params=_params(("arbitrary", "arbitrary")),
        name="ffn_attn" if attn is not None else "ffn",
    )(*args)


def _head_rms(x, g):
    outs = []
    for h in range(N_HEADS):
        outs.append(_rms(x[:, h * HEAD_DIM:(h + 1) * HEAD_DIM], g))
    return jnp.concatenate(outs, axis=-1)


def _split_heads(x):
    per_head = jnp.stack([x[:, h * HEAD_DIM:(h + 1) * HEAD_DIM] for h in range(N_HEADS)], axis=0)
    return jnp.transpose(per_head, (1, 0, 2))


def _merge_heads(x):
    per_head = jnp.transpose(x, (1, 0, 2))
    return jnp.concatenate([per_head[h] for h in range(N_HEADS)], axis=1)


def _qkv_kernel(x_ref, kvn_ref, wkv_ref, kn_ref, bn_ref, wq_ref, qn_ref,
                k_ref, v_ref, kb_ref, vb_ref, qb_ref):
    width = x_ref.shape[-1]
    x = x_ref[...]
    x = x * lax.rsqrt(jnp.mean(x * x, axis=-1, keepdims=True) + EPS)
    kv = _dot((x * kvn_ref[...]).astype(BF16), wkv_ref[...])
    k = _head_rms(kv[:, :width], kn_ref[...])
    v = kv[:, width:]
    k_ref[...] = _split_heads(k)
    v_ref[...] = _split_heads(v)
    kb_ref[...] = k.astype(BF16)
    vb_ref[...] = v.astype(BF16)
    q = _dot((x * bn_ref[...]).astype(BF16), wq_ref[...])
    qb_ref[...] = (_head_rms(q, qn_ref[...]) * SCORE_SCALE).astype(BF16)


def _qkv(x2d, kv_norm, w_kv, k_norm, b_norm, wq, q_norm, *, rows):
    n, width = x2d.shape
    blk = pl.BlockSpec((rows, width), lambda i: (i, 0))
    return pl.pallas_call(
        _qkv_kernel,
        grid=(n // rows,),
        in_specs=[blk, _const_spec(kv_norm.shape), _const_spec(w_kv.shape), _const_spec(k_norm.shape),
                  _const_spec(b_norm.shape), _const_spec(wq.shape), _const_spec(q_norm.shape)],
        out_specs=[pl.BlockSpec((rows, N_HEADS, HEAD_DIM), lambda i: (i, 0, 0))] * 2 + [blk] * 3,
        out_shape=[jax.ShapeDtypeStruct((n, N_HEADS, HEAD_DIM), F32)] * 2
        + [jax.ShapeDtypeStruct((n, width), BF16)] * 3,
        compiler_params=_params(("arbitrary",)),
        name="qkv",
    )(x2d, kv_norm, w_kv, k_norm, b_norm, wq, q_norm)


def _later_keys(n):
    j = lax.broadcasted_iota(jnp.int32, (n, n + LANES), 0)
    s = lax.broadcasted_iota(jnp.int32, (n, n + LANES), 1)
    return ((j > s) | (s >= n)).astype(BF16)


def _sb_tiles(qs, kts, vts, later, carry_ref, acc_ref, masks, firsts, slots=None):
    n = len(qs)
    slots = list(range(n)) if slots is None else slots
    log_betas, stays, ws = {}, {}, {}
    for step in range(n + 2 * ATT_SKEW):
        u = step
        if u < n:
            mask = masks[u]
            y = lax.dot_general(qs[u], kts[u], (((1,), (1,)), ((), ())), preferred_element_type=F32)
            log_beta = jnp.minimum(y, 0.0) - jnp.log(1.0 + jnp.exp2(-jnp.abs(y))) * LOG2_E
            log_stay = log_beta - y
            if mask is not None:
                log_stay = jnp.where(mask, log_stay, 0.0)
            log_betas[u] = log_beta
            stays[u] = log_stay.astype(BF16)
        u = step - ATT_SKEW
        if 0 <= u < n:
            mask = masks[u]
            sums = _dot(stays.pop(u), later)
            n_keys = sums.shape[1] - LANES
            after, total = sums[:, :n_keys], sums[:, n_keys:]
            if firsts[u]:
                w = jnp.exp2(log_betas.pop(u) + after)
                carry_ref[slots[u]] = total
            else:
                carry = carry_ref[slots[u]]
                w = jnp.exp2(log_betas.pop(u) + after + jnp.concatenate([carry] * (n_keys // LANES), axis=1))
                carry_ref[slots[u]] = carry + total
            if mask is not None:
                w = jnp.where(mask, w, 0.0)
            ws[u] = w.astype(BF16)
        u = step - 2 * ATT_SKEW
        if 0 <= u < n:
            pv = _dot(ws.pop(u), vts[u])
            acc_ref[slots[u]] = pv if firsts[u] else acc_ref[slots[u]] + pv


def _attn_prompt_kernel(q_ref, k_ref, v_ref, x_ref, wo_ref, y_ref, carry_s, acc_s):
    i = pl.program_id(2)
    n_q = ATT_Q_TILES
    tq = q_ref.shape[1] // n_q
    n_heads = q_ref.shape[2] // HEAD_DIM
    later = _later_keys(tq)
    row = lax.broadcasted_iota(jnp.int32, (tq, tq), 0)
    col = lax.broadcasted_iota(jnp.int32, (tq, tq), 1)
    diagonal = col < row
    heads = [slice(h * HEAD_DIM, (h + 1) * HEAD_DIM) for h in range(n_heads)]
    qs = [[q_ref[0, part * tq:(part + 1) * tq, hs] for hs in heads] for part in range(n_q)]

    def sweep(work):
        q_list, kts, vts, masks, firsts, slots = [], [], [], [], [], []
        for part, j, on_diagonal in work:
            start = pl.multiple_of(j * tq, tq)
            q_list += qs[part]
            kts += [k_ref[0, pl.ds(start, tq), hs] for hs in heads]
            vts += [v_ref[0, pl.ds(start, tq), hs] for hs in heads]
            masks += [diagonal if on_diagonal else None] * n_heads
            firsts += [on_diagonal] * n_heads
            slots += [part * n_heads + h for h in range(n_heads)]
        _sb_tiles(q_list, kts, vts, later, carry_s, acc_s, masks, firsts, slots=slots)

    sweep([(part, n_q * i + p, part == p) for p in reversed(range(n_q)) for part in range(p, n_q)])

    def earlier(n, _):
        j = n_q * i - 1 - ATT_KEY_TILES * n
        sweep([(part, j - k, False) for k in range(ATT_KEY_TILES) for part in range(n_q)])
        return 0

    lax.fori_loop(0, n_q * i // ATT_KEY_TILES, earlier, 0)
    o = jnp.concatenate([jnp.concatenate([acc_s[part * n_heads + h].astype(BF16) for h in range(n_heads)], axis=1)
                         for part in range(n_q)], axis=0)
    y_ref[0] = x_ref[0] + _dot(o, wo_ref[...])


def _attn_prompt(qb, kb, vb, x, wo):
    bsz, seq, width = qb.shape
    assert ATT_HEADS * HEAD_DIM == width
    tq = min(ATT_TILE, seq // ATT_Q_TILES)
    lanes = ATT_HEADS * HEAD_DIM
    grid = (bsz, width // lanes, seq // (ATT_Q_TILES * tq))
    q_spec = pl.BlockSpec((1, ATT_Q_TILES * tq, lanes), lambda b, h, i: (b, i, h))
    kv_spec = pl.BlockSpec((1, seq, lanes), lambda b, h, i: (b, 0, h))
    return pl.pallas_call(
        _attn_prompt_kernel,
        grid=grid,
        in_specs=[q_spec, kv_spec, kv_spec, q_spec, _const_spec(wo.shape)],
        out_specs=q_spec,
        out_shape=jax.ShapeDtypeStruct((bsz, seq, width), F32),
        scratch_shapes=[pltpu.VMEM((ATT_Q_TILES * ATT_HEADS, tq, LANES), F32),
                        pltpu.VMEM((ATT_Q_TILES * ATT_HEADS, tq, HEAD_DIM), F32)],
        compiler_params=_params(("arbitrary", "arbitrary", "arbitrary")),
        name="attn_prompt",
    )(qb, kb, vb, x, wo)


def _attn_sample_kernel(qbd_ref, kn_ref, vn_ref, kc_ref, vc_ref, o_ref, carry_s, acc_s, *, t_q):
    j = pl.program_id(1)
    n_j = pl.num_programs(1)
    rows = qbd_ref.shape[1]
    q = qbd_ref[0]

    @pl.when(j == 0)
    def _():
        n_new = kn_ref.shape[1]
        t = lax.broadcasted_iota(jnp.int32, (rows, n_new), 0) % t_q
        s = lax.broadcasted_iota(jnp.int32, (rows, n_new), 1)
        _sb_tiles([q], [kn_ref[0]], [vn_ref[0]], _later_keys(n_new), carry_s, acc_s, [s < t], [True])

    @pl.when(j > 0)
    def _():
        n_tiles = kc_ref.shape[1] // ATT_TILE
        later = _later_keys(ATT_TILE)

        def tile(n, _):
            start = pl.multiple_of((n_tiles - 1 - n) * ATT_TILE, ATT_TILE)
            kt = _merge_heads(kc_ref[0, pl.ds(start, ATT_TILE)]).astype(BF16)
            vt = _merge_heads(vc_ref[0, pl.ds(start, ATT_TILE)]).astype(BF16)
            _sb_tiles([q], [kt], [vt], later, carry_s, acc_s, [None], [False])
            return 0

        lax.fori_loop(0, n_tiles, tile, 0)

    @pl.when(j == n_j - 1)
    def _():
        for h in range(N_HEADS):
            o_ref[0, :, h * HEAD_DIM:(h + 1) * HEAD_DIM] = acc_s[
                0, h * t_q:(h + 1) * t_q, h * HEAD_DIM:(h + 1) * HEAD_DIM].astype(o_ref.dtype)


def _attn_sample(qbd, k_new, v_new, cache_k, cache_v, *, t_q, chunk):
    bsz, rows, width = qbd.shape
    past = cache_k.shape[1]
    n_chunks = past // chunk
    n_new = k_new.shape[1]
    cache_spec = pl.BlockSpec((1, chunk, N_HEADS, HEAD_DIM),
                              lambda b, j: (b, jnp.minimum(n_chunks - j, n_chunks - 1), 0, 0))
    new_spec = pl.BlockSpec((1, n_new, width), lambda b, j: (b, 0, 0))
    kern = functools.partial(_attn_sample_kernel, t_q=t_q)
    return pl.pallas_call(
        kern,
        grid=(bsz, n_chunks + 1),
        in_specs=[pl.BlockSpec((1, rows, width), lambda b, j: (b, 0, 0)), new_spec, new_spec,
                  cache_spec, cache_spec],
        out_specs=pl.BlockSpec((1, t_q, width), lambda b, j: (b, 0, 0)),
        out_shape=jax.ShapeDtypeStruct((bsz, t_q, width), BF16),
        scratch_shapes=[pltpu.VMEM((1, rows, LANES), F32), pltpu.VMEM((1, rows, width), F32)],
        compiler_params=_params(("arbitrary", "arbitrary")),
        name="attn_sample",
    )(qbd, k_new, v_new, cache_k, cache_v)


def _gate_weights(wr, wi):
    per = LRU_GROUP // wr.shape[-1]
    n_grp = wr.shape[0] // per
    eye = jnp.eye(per, dtype=wr.dtype)

    def expand(w):
        w = w.reshape(n_grp, per, w.shape[1], w.shape[2])
        return jnp.einsum('gnde,nm->gndme', w, eye).reshape(n_grp, LRU_GROUP, LRU_GROUP)

    return (0.5 * jnp.concatenate([expand(wr), expand(wi)], axis=-1)).astype(BF16)


def _block_diag_queries(qb, t_q):
    bsz = qb.shape[0]
    q = qb.reshape(bsz, t_q, N_HEADS, HEAD_DIM).transpose(0, 2, 1, 3)
    eye = jnp.eye(N_HEADS, dtype=qb.dtype)
    return jnp.einsum('bhtd,hg->bhtgd', q, eye).reshape(bsz, N_HEADS * t_q, N_HEADS * HEAD_DIM)


def _trunk(x, lru_h, lru_conv, ffn_conv, cache, w, *, n_streams, t_len, qkv_rows):
    bsz, seq, width = x.shape
    row = lambda v: v.reshape(1, -1)
    x1, h_new, c_new = _rglru(
        x, lru_h[0][:, None, :], lru_conv[0], row(w['a_norm'][0]), w['a_w_in'][0], w['a_conv_w'][0],
        row(w['a_conv_b'][0]), w['a_gate'], row(0.5 * w['a_br'][0]), row(0.5 * w['a_bi'][0]),
        row(w['a_lambda'][0]), w['a_w_out_half'][0], n_streams=n_streams, t_len=t_len)
    x2, f0_new = _ffn(x1, ffn_conv[0], row(w['f_norm'][0]), w['f_w_up'], w['f_conv_w'][0],
                      row(w['f_conv_b'][0]), w['f_w_down'], layer=0, n_streams=n_streams, t_len=t_len)
    k, v, kb, vb, qb = _qkv(x2.reshape(bsz * seq, width), row(w['kv_norm']), w['w_kv'], row(w['k_norm']),
                            row(w['b_norm'][0]), w['b_wq'][0], row(w['q_norm'][0]), rows=qkv_rows)
    shape3 = (bsz, seq, width)
    kb, vb, qb = kb.reshape(shape3), vb.reshape(shape3), qb.reshape(shape3)
    if cache is None:
        x3, attn = _attn_prompt(qb, kb, vb, x2, w['b_wo'][0]), None
    else:
        pad = ((0, 0), (0, ATT_NEW_TILE - seq), (0, 0))
        o = _attn_sample(_block_diag_queries(qb, seq), jnp.pad(kb, pad), jnp.pad(vb, pad),
                         cache[0], cache[1], t_q=seq, chunk=CACHE_CHUNK)
        x3, attn = x2, (o, w['b_wo'][0])
    y, f1_new = _ffn(x3, ffn_conv[1], row(w['f_norm'][1]), w['f_w_up'], w['f_conv_w'][1],
                     row(w['f_conv_b'][1]), w['f_w_down'], layer=1, n_streams=n_streams, t_len=t_len,
                     attn=attn, out_norm=row(w['out_norm']))
    heads = (bsz, seq, N_HEADS, HEAD_DIM)
    return (y, h_new.reshape(1, bsz, width), c_new[None], jnp.stack([f0_new, f1_new]),
            k.reshape(heads), v.reshape(heads))


def kernel(x_prompt, x_sample, state_lru_h, state_lru_conv, state_ffn_conv, cache_k, cache_v, a_norm, a_w_in, a_conv_w, a_conv_b, a_wr, a_br, a_wi, a_bi, a_lambda, a_w_out, kv_norm, w_kv, k_norm, b_norm, b_wq, q_norm, b_wo, f_norm, f_w_up, f_conv_w, f_conv_b, f_w_down, out_norm):
    w = dict(a_norm=a_norm, a_w_in=a_w_in.astype(BF16), a_conv_w=a_conv_w, a_conv_b=a_conv_b,
             a_gate=_gate_weights(a_wr[0], a_wi[0]), a_br=a_br, a_bi=a_bi, a_lambda=a_lambda,
             a_w_out_half=(0.5 * a_w_out).astype(BF16), kv_norm=kv_norm, w_kv=w_kv.astype(BF16), k_norm=k_norm,
             b_norm=b_norm, b_wq=b_wq.astype(BF16), q_norm=q_norm, b_wo=b_wo.astype(BF16),
             f_norm=f_norm, f_w_up=f_w_up.astype(BF16), f_conv_w=f_conv_w, f_conv_b=f_conv_b,
             f_w_down=f_w_down.astype(BF16), out_norm=out_norm)
    bp, sp, width = x_prompt.shape
    d_ff = f_w_down.shape[1]
    n_lru = a_w_in.shape[0]
    depth = f_w_up.shape[0]
    zeros = lambda *s: jnp.zeros(s, x_prompt.dtype)
    prompt = _trunk(x_prompt, zeros(n_lru, bp, width), zeros(n_lru, bp, LRU_CONV - 1, width),
                    zeros(depth, bp, FFN_CONV - 1, d_ff), None, w,
                    n_streams=1, t_len=min(ROW_BLOCK, sp), qkv_rows=min(QKV_ROWS, bp * sp))
    bs, ss, _ = x_sample.shape
    sample = _trunk(x_sample, state_lru_h, state_lru_conv, state_ffn_conv, (cache_k, cache_v), w,
                    n_streams=bs, t_len=ss, qkv_rows=bs * ss)
    y_p, h_p, c_p, f_p, k_p, v_p = prompt
    y_s, h_s, c_s, f_s, k_s, v_s = sample
    return (y_p, y_s, h_p, c_p, f_p, k_p, v_p, h_s, c_s, f_s, k_s, v_s)
```
